```python
import jax, jax.numpy as jnp
from jax import lax
import numpy as np

D_MODEL = 1024
BATCH = 8
SEQ = 8192
DEPTH = 2

MIX_WIDTH = D_MODEL
WIDTH_A = MIX_WIDTH // 2
WIDTH_B = MIX_WIDTH - WIDTH_A
HEADS_A = 4
HEAD_DIM_A = WIDTH_A // HEADS_A
GROUPS_B = 4
CHUNK = 128
CONV_K = 3
PLE_DIM = 256
EPS = 1e-6
SPLITS = [WIDTH_A, WIDTH_A, WIDTH_A, WIDTH_B, WIDTH_B, WIDTH_B, WIDTH_B]
PROJ_WIDTH = sum(SPLITS)

kernel_name = "hybrid_sgu_shortconv_ple_trunk"


def rmsnorm(x, g):
    xf = x.astype(jnp.float32)
    y = xf * lax.rsqrt(jnp.mean(xf * xf, axis=-1, keepdims=True) + EPS)
    return (y * g.astype(jnp.float32)).astype(x.dtype)


def layernorm(x, g, b):
    xf = x.astype(jnp.float32)
    mu = jnp.mean(xf, axis=-1, keepdims=True)
    xc = xf - mu
    var = jnp.mean(xc * xc, axis=-1, keepdims=True)
    y = xc * lax.rsqrt(var + EPS)
    return (y * g.astype(jnp.float32) + b.astype(jnp.float32)).astype(x.dtype)


def spatial_gating(u, v, ln_g, ln_b, w_s, b_s):
    bsz, s_len, _ = v.shape
    n_chunks = s_len // CHUNK
    v = layernorm(v, ln_g, ln_b)
    vh = v.reshape(bsz, n_chunks, CHUNK, HEADS_A, HEAD_DIM_A)
    mask = jnp.tril(jnp.ones((CHUNK, CHUNK), dtype=w_s.dtype))
    ws = w_s * mask[None]
    mixed = jnp.einsum('hts,bnshd->bnthd', ws, vh)
    mixed = mixed + jnp.transpose(b_s)[None, None, :, :, None]
    return u * mixed.reshape(bsz, s_len, WIDTH_A)


def gated_short_conv(h, gate_b, gate_c, conv_w):
    s_len = h.shape[1]
    xc = gate_c * h
    xp = jnp.pad(xc, ((0, 0), (CONV_K - 1, 0), (0, 0)))
    y = xp[:, 0:s_len] * conv_w[:, 0]
    for k in range(1, CONV_K):
        y = y + xp[:, k:k + s_len] * conv_w[:, k]
    return gate_b * y


def setup_inputs(seed: int = 0) -> dict:
    key = jax.random.key(seed)
    ks = jax.random.split(key, 16)
    f32 = jnp.float32
    x = jax.random.normal(ks[0], (BATCH, SEQ, D_MODEL), f32)
    p = jax.random.normal(ks[1], (DEPTH, BATCH, SEQ, PLE_DIM), f32)
    norm_g = 1.0 + 0.02 * jax.random.normal(ks[2], (DEPTH, D_MODEL), f32)
    w_in = jax.random.normal(ks[3], (DEPTH, D_MODEL, PROJ_WIDTH), f32) * D_MODEL ** -0.5
    ln_v_g = 1.0 + 0.02 * jax.random.normal(ks[4], (DEPTH, WIDTH_A), f32)
    ln_v_b = 0.02 * jax.random.normal(ks[5], (DEPTH, WIDTH_A), f32)
    w_s = 0.5 * jax.random.normal(ks[6], (DEPTH, HEADS_A, CHUNK, CHUNK), f32) * CHUNK ** -0.5
    b_s = 1.0 + 0.1 * jax.random.normal(ks[7], (DEPTH, HEADS_A, CHUNK), f32)
    conv_w = jax.random.normal(ks[8], (DEPTH, WIDTH_B, CONV_K), f32) * CONV_K ** -0.5
    w_out = jax.random.normal(ks[9], (DEPTH, MIX_WIDTH, D_MODEL), f32) * MIX_WIDTH ** -0.5
    ple_norm_g = 1.0 + 0.02 * jax.random.normal(ks[10], (DEPTH, D_MODEL), f32)
    w_ple_gate = jax.random.normal(ks[11], (DEPTH, D_MODEL, D_MODEL), f32) * D_MODEL ** -0.5
    w_ple_proj = 0.5 * jax.random.normal(ks[12], (DEPTH, PLE_DIM, D_MODEL), f32) * PLE_DIM ** -0.5
    final_g = 1.0 + 0.02 * jax.random.normal(ks[13], (D_MODEL,), f32)
    return {"x": x, "p": p, "norm_g": norm_g, "w_in": w_in, "ln_v_g": ln_v_g,
            "ln_v_b": ln_v_b, "w_s": w_s, "b_s": b_s, "conv_w": conv_w,
            "w_out": w_out, "ple_norm_g": ple_norm_g, "w_ple_gate": w_ple_gate,
            "w_ple_proj": w_ple_proj, "final_g": final_g}


def reference(x, p, norm_g, w_in, ln_v_g, ln_v_b, w_s, b_s, conv_w, w_out,
              ple_norm_g, w_ple_gate, w_ple_proj, final_g):
    split_idx = list(np.cumsum(SPLITS)[:-1])
    for i in range(DEPTH):
        hn = rmsnorm(x, norm_g[i])
        proj = hn @ w_in[i]
        u_a, v_a, z_a, h_b, gb, gc, z_b = jnp.split(proj, split_idx, axis=-1)
        out_a = spatial_gating(u_a, v_a, ln_v_g[i], ln_v_b[i], w_s[i], b_s[i]) * jax.nn.silu(z_a)
        out_b = gated_short_conv(h_b, gb, gc, conv_w[i]) * jax.nn.silu(z_b)
        x = x + jnp.concatenate([out_a, out_b], axis=-1) @ w_out[i]
        gate = jax.nn.sigmoid(rmsnorm(x, ple_norm_g[i]) @ w_ple_gate[i])
        x = x + gate * (p[i] @ w_ple_proj[i])
    return rmsnorm(x, final_g)
```

```python
import functools

import jax
import jax.numpy as jnp
from jax import lax
from jax.experimental import pallas as pl
from jax.experimental.pallas import tpu as pltpu

D_MODEL = 1024
WIDTH_A = 512
WIDTH_B = 512
HEADS_A = 4
HEAD_DIM_A = WIDTH_A // HEADS_A
CHUNK = 128
CONV_K = 3
PLE_DIM = 256
EPS = 1e-6
OFF_U, OFF_V, OFF_ZA, OFF_H, OFF_GB, OFF_GC, OFF_ZB = (
    0, WIDTH_A, 2 * WIDTH_A, 3 * WIDTH_A, 3 * WIDTH_A + WIDTH_B,
    3 * WIDTH_A + 2 * WIDTH_B, 3 * WIDTH_A + 3 * WIDTH_B)
PROJ_WIDTH = 3 * WIDTH_A + 4 * WIDTH_B

V7X_SUBLANES = 8
V7X_VMEM_LIMIT_BYTES = 56 * 1024 * 1024
SEQ_TILE = 512

_F32 = jnp.float32
_BF16 = jnp.bfloat16


def _rmsnorm(x, g):
    return x * lax.rsqrt(jnp.mean(x * x, axis=-1, keepdims=True) + EPS) * g


def _silu(z):
    return z * jax.nn.sigmoid(z)


def _layer_kernel(x_ref, p_ref, norm_g_ref, w_in_ref, ln_g_ref, ln_b_ref, ws_ref, bias_ref,
                  conv_w_ref, w_out_ref, ple_g_ref, w_gate_ref, w_proj_ref, final_g_ref,
                  o_ref, act_ref, xc_ref, *, tm, is_last):
    @pl.when(pl.program_id(1) == 0)
    def _():
        xc_ref[0:V7X_SUBLANES, :] = jnp.zeros((V7X_SUBLANES, WIDTH_B), _F32)

    x = x_ref[0]
    hn = _rmsnorm(x, norm_g_ref[...]).astype(_BF16)

    def proj(off, width):
        return jnp.dot(hn, w_in_ref[:, off:off + width], preferred_element_type=_F32)

    v = proj(OFF_V, WIDTH_A)
    mu = jnp.mean(v, axis=-1, keepdims=True)
    vc = v - mu
    var = jnp.mean(vc * vc, axis=-1, keepdims=True)
    v_ln = (vc * lax.rsqrt(var + EPS) * ln_g_ref[...] + ln_b_ref[...]).astype(_BF16)
    tril = (lax.broadcasted_iota(jnp.int32, (CHUNK, CHUNK), 0)
            >= lax.broadcasted_iota(jnp.int32, (CHUNK, CHUNK), 1))
    ws = [jnp.where(tril, ws_ref[h], 0.0).astype(_BF16) for h in range(HEADS_A)]
    gate_a = proj(OFF_U, WIDTH_A) * _silu(proj(OFF_ZA, WIDTH_A))
    for c in range(tm // CHUNK):
        rows = slice(c * CHUNK, (c + 1) * CHUNK)
        for h in range(HEADS_A):
            cols = slice(h * HEAD_DIM_A, (h + 1) * HEAD_DIM_A)
            mixed = jnp.dot(ws[h], v_ln[rows, cols], preferred_element_type=_F32)
            mixed = mixed + bias_ref[:, cols]
            act_ref[rows, cols] = (gate_a[rows, cols] * mixed).astype(_BF16)

    xc_ref[V7X_SUBLANES:V7X_SUBLANES + tm, :] = proj(OFF_GC, WIDTH_B) * proj(OFF_H, WIDTH_B)
    y = xc_ref[V7X_SUBLANES - 2:V7X_SUBLANES - 2 + tm, :] * conv_w_ref[0:1, :]
    y = y + xc_ref[V7X_SUBLANES - 1:V7X_SUBLANES - 1 + tm, :] * conv_w_ref[1:2, :]
    y = y + xc_ref[V7X_SUBLANES:V7X_SUBLANES + tm, :] * conv_w_ref[2:3, :]
    xc_ref[0:V7X_SUBLANES, :] = xc_ref[tm:tm + V7X_SUBLANES, :]
    out_b = proj(OFF_GB, WIDTH_B) * y * _silu(proj(OFF_ZB, WIDTH_B))
    act_ref[:, WIDTH_A:] = out_b.astype(_BF16)

    x1 = x + jnp.dot(act_ref[...], w_out_ref[...], preferred_element_type=_F32)

    gn = _rmsnorm(x1, ple_g_ref[...]).astype(_BF16)
    gate = jax.nn.sigmoid(jnp.dot(gn, w_gate_ref[...], preferred_element_type=_F32))
    pe = jnp.dot(p_ref[0, 0].astype(_BF16), w_proj_ref[...], preferred_element_type=_F32)
    x2 = x1 + gate * pe
    if is_last:
        x2 = _rmsnorm(x2, final_g_ref[...])
    o_ref[0] = x2


def _layer_call(x, p, layer, weights, final_g, *, is_last):
    bsz, seq, _ = x.shape
    tm = SEQ_TILE
    assert seq % tm == 0 and tm % CHUNK == 0

    def resident(shape):
        return pl.BlockSpec((None,) + shape, lambda b, s: (layer,) + (0,) * len(shape),
                            pipeline_mode=pl.Buffered(1))

    in_specs = [
        pl.BlockSpec((1, tm, D_MODEL), lambda b, s: (b, s, 0)),
        pl.BlockSpec((1, 1, tm, PLE_DIM), lambda b, s: (layer, b, s, 0)),
        resident((1, D_MODEL)),
        resident((D_MODEL, PROJ_WIDTH)),
        resident((1, WIDTH_A)),
        resident((1, WIDTH_A)),
        resident((HEADS_A, CHUNK, CHUNK)),
        resident((CHUNK, WIDTH_A)),
        resident((CONV_K, WIDTH_B)),
        resident((D_MODEL, D_MODEL)),
        resident((1, D_MODEL)),
        resident((D_MODEL, D_MODEL)),
        resident((PLE_DIM, D_MODEL)),
        pl.BlockSpec((1, D_MODEL), lambda b, s: (0, 0), pipeline_mode=pl.Buffered(1)),
    ]
    return pl.pallas_call(
        functools.partial(_layer_kernel, tm=tm, is_last=is_last),
        grid=(bsz, seq // tm),
        in_specs=in_specs,
        out_specs=pl.BlockSpec((1, tm, D_MODEL), lambda b, s: (b, s, 0)),
        out_shape=jax.ShapeDtypeStruct(x.shape, x.dtype),
        scratch_shapes=[
            pltpu.VMEM((tm, D_MODEL), _BF16),
            pltpu.VMEM((tm + V7X_SUBLANES, WIDTH_B), _F32),
        ],
        compiler_params=pltpu.CompilerParams(
            dimension_semantics=("arbitrary", "arbitrary"),
            vmem_limit_bytes=V7X_VMEM_LIMIT_BYTES),
        name=f"trunk_layer{layer}",
    )(x, p, *weights, final_g)


def kernel(x, p, norm_g, w_in, ln_v_g, ln_v_b, w_s, b_s, conv_w, w_out, ple_norm_g,
           w_ple_gate, w_ple_proj, final_g):
    depth = w_in.shape[0]
    bias = jnp.repeat(jnp.swapaxes(b_s, 1, 2), HEAD_DIM_A, axis=2)
    weights = (
        norm_g[:, None, :], w_in.astype(_BF16), ln_v_g[:, None, :], ln_v_b[:, None, :],
        w_s, bias, jnp.swapaxes(conv_w, 1, 2), w_out.astype(_BF16),
        ple_norm_g[:, None, :], w_ple_gate.astype(_BF16), w_ple_proj.astype(_BF16))
    for layer in range(depth):
        x = _layer_call(x, p, layer, weights, final_g[None, :], is_last=layer == depth - 1)
    return x
```

```python
import functools

import jax
import jax.numpy as jnp
from jax import lax
from jax.experimental import pallas as pl
from jax.experimental.pallas import tpu as pltpu

D_MODEL = 1024
WIDTH_A = 512
WIDTH_B = 512
HEADS_A = 4
HEAD_DIM_A = WIDTH_A // HEADS_A
CHUNK = 128
CONV_K = 3
PLE_DIM = 256
EPS = 1e-6
OFF_U, OFF_V, OFF_ZA, OFF_H, OFF_GB, OFF_GC, OFF_ZB = (
    0, WIDTH_A, 2 * WIDTH_A, 3 * WIDTH_A, 3 * WIDTH_A + WIDTH_B,
    3 * WIDTH_A + 2 * WIDTH_B, 3 * WIDTH_A + 3 * WIDTH_B)
PROJ_WIDTH = 3 * WIDTH_A + 4 * WIDTH_B

V7X_SUBLANES = 8
V7X_VMEM_LIMIT_BYTES = 56 * 1024 * 1024
SEQ_TILE = 1024
SUB_TILE = 256

_F32 = jnp.float32
_BF16 = jnp.bfloat16


def _rmsnorm(x, g):
    return x * lax.rsqrt(jnp.mean(x * x, axis=-1, keepdims=True) + EPS) * g


def _silu(z):
    return z * jax.nn.sigmoid(z)


def _layer_kernel(x_ref, p_ref, norm_g_ref, w_in_ref, ln_g_ref, ln_b_ref, ws_ref, bias_ref,
                  conv_w_ref, w_out_ref, ple_g_ref, w_gate_ref, w_proj_ref, final_g_ref,
                  o_ref, act_ref, xc_ref, *, tm, sub, is_last):
    @pl.when(pl.program_id(1) == 0)
    def _():
        xc_ref[0:V7X_SUBLANES, :] = jnp.zeros((V7X_SUBLANES, WIDTH_B), _F32)

    tril = (lax.broadcasted_iota(jnp.int32, (CHUNK, CHUNK), 0)
            >= lax.broadcasted_iota(jnp.int32, (CHUNK, CHUNK), 1))
    ws = [jnp.where(tril, ws_ref[h], 0.0).astype(_BF16) for h in range(HEADS_A)]

    n_sub = tm // sub
    st = [{} for _ in range(n_sub)]

    def norm_in(j):
        s = st[j]
        s["x"] = x_ref[0, j * sub:(j + 1) * sub, :]
        s["hn"] = _rmsnorm(s["x"], norm_g_ref[...]).astype(_BF16)

    def project(j):
        s, r0 = st[j], j * sub

        def proj(off, width):
            return jnp.dot(s["hn"], w_in_ref[:, off:off + width],
                           preferred_element_type=_F32)

        v = proj(OFF_V, WIDTH_A)
        mu = jnp.mean(v, axis=-1, keepdims=True)
        vc = v - mu
        var = jnp.mean(vc * vc, axis=-1, keepdims=True)
        s["v_ln"] = (vc * lax.rsqrt(var + EPS) * ln_g_ref[...] + ln_b_ref[...]).astype(_BF16)
        s["gate_a"] = proj(OFF_U, WIDTH_A) * _silu(proj(OFF_ZA, WIDTH_A))
        c0 = V7X_SUBLANES + r0
        xc_ref[c0:c0 + sub, :] = proj(OFF_GC, WIDTH_B) * proj(OFF_H, WIDTH_B)
        y = xc_ref[c0 - 2:c0 - 2 + sub, :] * conv_w_ref[0:1, :]
        y = y + xc_ref[c0 - 1:c0 - 1 + sub, :] * conv_w_ref[1:2, :]
        y = y + xc_ref[c0:c0 + sub, :] * conv_w_ref[2:3, :]
        out_b = proj(OFF_GB, WIDTH_B) * y * _silu(proj(OFF_ZB, WIDTH_B))
        act_ref[r0:r0 + sub, WIDTH_A:] = out_b.astype(_BF16)

    def spatial(j):
        s, r0 = st[j], j * sub
        for c in range(sub // CHUNK):
            rows = slice(c * CHUNK, (c + 1) * CHUNK)
            for h in range(HEADS_A):
                cols = slice(h * HEAD_DIM_A, (h + 1) * HEAD_DIM_A)
                mixed = jnp.dot(ws[h], s["v_ln"][rows, cols], preferred_element_type=_F32)
                mixed = mixed + bias_ref[:, cols]
                act_ref[r0 + c * CHUNK:r0 + (c + 1) * CHUNK, cols] = (
                    s["gate_a"][rows, cols] * mixed).astype(_BF16)

    def out_project(j):
        s, r0 = st[j], j * sub
        s["x1"] = s["x"] + jnp.dot(act_ref[r0:r0 + sub, :], w_out_ref[...],
                                   preferred_element_type=_F32)
        s["gn"] = _rmsnorm(s["x1"], ple_g_ref[...]).astype(_BF16)

    def embed(j):
        s, r0 = st[j], j * sub
        pe = jnp.dot(p_ref[0, 0, r0:r0 + sub, :].astype(_BF16), w_proj_ref[...],
                     preferred_element_type=_F32)
        gate = jax.nn.sigmoid(jnp.dot(s["gn"], w_gate_ref[...], preferred_element_type=_F32))
        x2 = s["x1"] + gate * pe
        if is_last:
            x2 = _rmsnorm(x2, final_g_ref[...])
        o_ref[0, r0:r0 + sub, :] = x2

    norm_in(0)
    project(0)
    for k in range(1, n_sub):
        norm_in(k)
        spatial(k - 1)
        project(k)
        out_project(k - 1)
        if k >= 2:
            embed(k - 2)
    spatial(n_sub - 1)
    if n_sub >= 2:
        embed(n_sub - 2)
    out_project(n_sub - 1)
    embed(n_sub - 1)

    xc_ref[0:V7X_SUBLANES, :] = xc_ref[tm:tm + V7X_SUBLANES, :]


def _layer_call(x, p, layer, weights, final_g, *, is_last):
    bsz, seq, _ = x.shape
    tm = SEQ_TILE
    assert seq % tm == 0 and tm % SUB_TILE == 0 and SUB_TILE % CHUNK == 0

    def resident(shape):
        return pl.BlockSpec((None,) + shape, lambda b, s: (layer,) + (0,) * len(shape),
                            pipeline_mode=pl.Buffered(1))

    in_specs = [
        pl.BlockSpec((1, tm, D_MODEL), lambda b, s: (b, s, 0)),
        pl.BlockSpec((1, 1, tm, PLE_DIM), lambda b, s: (layer, b, s, 0)),
        resident((1, D_MODEL)),
        resident((D_MODEL, PROJ_WIDTH)),
        resident((1, WIDTH_A)),
        resident((1, WIDTH_A)),
        resident((HEADS_A, CHUNK, CHUNK)),
        resident((CHUNK, WIDTH_A)),
        resident((CONV_K, WIDTH_B)),
        resident((D_MODEL, D_MODEL)),
        resident((1, D_MODEL)),
        resident((D_MODEL, D_MODEL)),
        resident((PLE_DIM, D_MODEL)),
        pl.BlockSpec((1, D_MODEL), lambda b, s: (0, 0), pipeline_mode=pl.Buffered(1)),
    ]
    return pl.pallas_call(
        functools.partial(_layer_kernel, tm=tm, sub=SUB_TILE, is_last=is_last),
        grid=(bsz, seq // tm),
        in_specs=in_specs,
        out_specs=pl.BlockSpec((1, tm, D_MODEL), lambda b, s: (b, s, 0)),
        out_shape=jax.ShapeDtypeStruct(x.shape, x.dtype),
        scratch_shapes=[
            pltpu.VMEM((tm, D_MODEL), _BF16),
            pltpu.VMEM((tm + V7X_SUBLANES, WIDTH_B), _F32),
        ],
        compiler_params=pltpu.CompilerParams(
            dimension_semantics=("arbitrary", "arbitrary"),
            vmem_limit_bytes=V7X_VMEM_LIMIT_BYTES),
        name=f"trunk_layer{layer}",
    )(x, p, *weights, final_g)


def kernel(x, p, norm_g, w_in, ln_v_g, ln_v_b, w_s, b_s, conv_w, w_out, ple_norm_g,
           w_ple_gate, w_ple_proj, final_g):
    depth = w_in.shape[0]
    bias = jnp.repeat(jnp.swapaxes(b_s, 1, 2), HEAD_DIM_A, axis=2)
    weights = (
        norm_g[:, None, :], w_in.astype(_BF16), ln_v_g[:, None, :], ln_v_b[:, None, :],
        w_s, bias, jnp.swapaxes(conv_w, 1, 2), w_out.astype(_BF16),
        ple_norm_g[:, None, :], w_ple_gate.astype(_BF16), w_ple_proj.astype(_BF16))
    for layer in range(depth):
        x = _layer_call(x, p, layer, weights, final_g[None, :], is_last=layer == depth - 1)
    return x
```

```python
import functools

import jax
import jax.numpy as jnp
from jax import lax
from jax.experimental import pallas as pl
from jax.experimental.pallas import tpu as pltpu

D_MODEL = 1024
WIDTH_A = 512
WIDTH_B = 512
HEADS_A = 4
HEAD_DIM_A = WIDTH_A // HEADS_A
CHUNK = 128
CONV_K = 3
PLE_DIM = 256
EPS = 1e-6
OFF_U, OFF_V, OFF_ZA, OFF_H, OFF_GB, OFF_GC, OFF_ZB = (
    0, WIDTH_A, 2 * WIDTH_A, 3 * WIDTH_A, 3 * WIDTH_A + WIDTH_B,
    3 * WIDTH_A + 2 * WIDTH_B, 3 * WIDTH_A + 3 * WIDTH_B)
PROJ_WIDTH = 3 * WIDTH_A + 4 * WIDTH_B

V7X_SUBLANES = 8
V7X_VMEM_LIMIT_BYTES = 56 * 1024 * 1024
SEQ_TILE = 1024
SUB_TILE = 256

_F32 = jnp.float32
_BF16 = jnp.bfloat16


def _rmsnorm(x, g):
    return x * lax.rsqrt(jnp.mean(x * x, axis=-1, keepdims=True) + EPS) * g


def _silu(z):
    return z * jax.nn.sigmoid(z)


def _shift_rows(cur, prev_tail, k):
    rolled = pltpu.roll(cur, k, axis=0)
    head = pltpu.roll(prev_tail, k, axis=0)
    row = lax.broadcasted_iota(jnp.int32, head.shape, 0)
    first = jnp.where(row < k, head, rolled[0:V7X_SUBLANES])
    return jnp.concatenate([first, rolled[V7X_SUBLANES:]], axis=0)


def _layer_kernel(x_ref, p_ref, norm_g_ref, w_in_ref, ln_g_ref, ln_b_ref, ws_ref, bias_ref,
                  conv_w_ref, w_out_ref, ple_g_ref, w_gate_ref, w_proj_ref, final_g_ref,
                  o_ref, xc_ref, *, tm, sub, is_last):
    @pl.when(pl.program_id(1) == 0)
    def _():
        xc_ref[...] = jnp.zeros((V7X_SUBLANES, WIDTH_B), _F32)

    tril = (lax.broadcasted_iota(jnp.int32, (CHUNK, CHUNK), 0)
            >= lax.broadcasted_iota(jnp.int32, (CHUNK, CHUNK), 1))
    ws = [jnp.where(tril, ws_ref[h], 0.0).astype(_BF16) for h in range(HEADS_A)]

    n_sub = tm // sub
    st = [{} for _ in range(n_sub)]

    def norm_in(j):
        s = st[j]
        s["x"] = x_ref[0, j * sub:(j + 1) * sub, :]
        s["hn"] = _rmsnorm(s["x"], norm_g_ref[...]).astype(_BF16)

    def project(j):
        s = st[j]

        def proj(off, width):
            return jnp.dot(s["hn"], w_in_ref[:, off:off + width],
                           preferred_element_type=_F32)

        v = proj(OFF_V, WIDTH_A)
        mu = jnp.mean(v, axis=-1, keepdims=True)
        vc = v - mu
        var = jnp.mean(vc * vc, axis=-1, keepdims=True)
        s["v_ln"] = (vc * lax.rsqrt(var + EPS) * ln_g_ref[...] + ln_b_ref[...]).astype(_BF16)
        xc = proj(OFF_GC, WIDTH_B) * proj(OFF_H, WIDTH_B)
        prev_tail = xc_ref[...] if j == 0 else st[j - 1]["xc_tail"]
        s["xc_tail"] = xc[sub - V7X_SUBLANES:, :]
        y = _shift_rows(xc, prev_tail, 2) * conv_w_ref[0:1, :]
        y = y + _shift_rows(xc, prev_tail, 1) * conv_w_ref[1:2, :]
        y = y + xc * conv_w_ref[2:3, :]
        out_b = proj(OFF_GB, WIDTH_B) * y * _silu(proj(OFF_ZB, WIDTH_B))
        s["out_b"] = out_b.astype(_BF16)
        silu_za = _silu(proj(OFF_ZA, WIDTH_A))
        s["gate_a"] = proj(OFF_U, WIDTH_A) * silu_za

    def spatial(j):
        s = st[j]
        chunks = []
        for c in range(sub // CHUNK):
            rows = slice(c * CHUNK, (c + 1) * CHUNK)
            heads = []
            for h in range(HEADS_A):
                cols = slice(h * HEAD_DIM_A, (h + 1) * HEAD_DIM_A)
                mixed = jnp.dot(ws[h], s["v_ln"][rows, cols], preferred_element_type=_F32)
                mixed = mixed + bias_ref[:, cols]
                heads.append((s["gate_a"][rows, cols] * mixed).astype(_BF16))
            chunks.append(jnp.concatenate(heads, axis=1))
        s["out_a"] = jnp.concatenate(chunks, axis=0)

    def out_project(j):
        s = st[j]
        act = jnp.concatenate([s["out_a"], s["out_b"]], axis=1)
        s["x1"] = s["x"] + jnp.dot(act, w_out_ref[...], preferred_element_type=_F32)
        s["gn"] = _rmsnorm(s["x1"], ple_g_ref[...]).astype(_BF16)

    def embed_project(j):
        r0 = j * sub
        st[j]["pe"] = jnp.dot(p_ref[0, 0, r0:r0 + sub, :].astype(_BF16), w_proj_ref[...],
                              preferred_element_type=_F32)

    def embed_gate(j):
        s, r0 = st[j], j * sub
        gate = jax.nn.sigmoid(jnp.dot(s["gn"], w_gate_ref[...], preferred_element_type=_F32))
        x2 = s["x1"] + gate * s["pe"]
        if is_last:
            x2 = _rmsnorm(x2, final_g_ref[...])
        o_ref[0, r0:r0 + sub, :] = x2

    n_lead = min(2, n_sub)
    norm_in(0)
    for j in range(n_lead):
        embed_project(j)
    project(0)
    for k in range(1, n_sub):
        norm_in(k)
        spatial(k - 1)
        project(k)
        out_project(k - 1)
        if k >= 2:
            embed_gate(k - 2)
    spatial(n_sub - 1)
    for j in range(n_lead, n_sub):
        embed_project(j)
    out_project(n_sub - 1)
    if n_sub >= 2:
        embed_gate(n_sub - 2)
    embed_gate(n_sub - 1)

    xc_ref[...] = st[n_sub - 1]["xc_tail"]


def _layer_call(x, p, layer, weights, final_g, *, is_last):
    bsz, seq, _ = x.shape
    tm = SEQ_TILE
    assert seq % tm == 0 and tm % SUB_TILE == 0 and SUB_TILE % CHUNK == 0

    def resident(shape):
        return pl.BlockSpec((None,) + shape, lambda b, s: (layer,) + (0,) * len(shape),
                            pipeline_mode=pl.Buffered(1))

    in_specs = [
        pl.BlockSpec((1, tm, D_MODEL), lambda b, s: (b, s, 0)),
        pl.BlockSpec((1, 1, tm, PLE_DIM), lambda b, s: (layer, b, s, 0)),
        resident((1, D_MODEL)),
        resident((D_MODEL, PROJ_WIDTH)),
        resident((1, WIDTH_A)),
        resident((1, WIDTH_A)),
        resident((HEADS_A, CHUNK, CHUNK)),
        resident((CHUNK, WIDTH_A)),
        resident((CONV_K, WIDTH_B)),
        resident((D_MODEL, D_MODEL)),
        resident((1, D_MODEL)),
        resident((D_MODEL, D_MODEL)),
        resident((PLE_DIM, D_MODEL)),
        pl.BlockSpec((1, D_MODEL), lambda b, s: (0, 0), pipeline_mode=pl.Buffered(1)),
    ]
    return pl.pallas_call(
        functools.partial(_layer_kernel, tm=tm, sub=SUB_TILE, is_last=is_last),
        grid=(bsz, seq // tm),
        in_specs=in_specs,
        out_specs=pl.BlockSpec((1, tm, D_MODEL), lambda b, s: (b, s, 0)),
        out_shape=jax.ShapeDtypeStruct(x.shape, x.dtype),
        scratch_shapes=[
            pltpu.VMEM((V7X_SUBLANES, WIDTH_B), _F32),
        ],
        compiler_params=pltpu.CompilerParams(
            dimension_semantics=("arbitrary", "arbitrary"),
            vmem_limit_bytes=V7X_VMEM_LIMIT_BYTES),
        name=f"trunk_layer{layer}",
    )(x, p, *weights, final_g)


def kernel(x, p, norm_g, w_in, ln_v_g, ln_v_b, w_s, b_s, conv_w, w_out, ple_norm_g,
           w_ple_gate, w_ple_proj, final_g):
    depth = w_in.shape[0]
    bias = jnp.repeat(jnp.swapaxes(b_s, 1, 2), HEAD_DIM_A, axis=2)
    weights = (
        norm_g[:, None, :], w_in.astype(_BF16), ln_v_g[:, None, :], ln_v_b[:, None, :],
        w_s, bias, jnp.swapaxes(conv_w, 1, 2), w_out.astype(_BF16),
        ple_norm_g[:, None, :], w_ple_gate.astype(_BF16), w_ple_proj.astype(_BF16))
    for layer in range(depth):
        x = _layer_call(x, p, layer, weights, final_g[None, :], is_last=layer == depth - 1)
    return x
```

```python
import functools

import jax
import jax.numpy as jnp
from jax import lax
from jax.experimental import pallas as pl
from jax.experimental.pallas import tpu as pltpu

D_MODEL = 1024
WIDTH_A = 512
WIDTH_B = 512
HEADS_A = 4
HEAD_DIM_A = WIDTH_A // HEADS_A
CHUNK = 128
CONV_K = 3
PLE_DIM = 256
EPS = 1e-6
OFF_U, OFF_V, OFF_ZA, OFF_H, OFF_GB, OFF_GC, OFF_ZB = (
    0, WIDTH_A, 2 * WIDTH_A, 3 * WIDTH_A, 3 * WIDTH_A + WIDTH_B,
    3 * WIDTH_A + 2 * WIDTH_B, 3 * WIDTH_A + 3 * WIDTH_B)
PROJ_WIDTH = 3 * WIDTH_A + 4 * WIDTH_B
GROUPED_OFF_V = 0
GROUPED_OFF_A = WIDTH_A
GROUPED_OFF_B = 3 * WIDTH_A

V7X_SUBLANES = 8
V7X_VMEM_LIMIT_BYTES = 56 * 1024 * 1024
SEQ_TILE = 1024
SUB_TILE = 256

_F32 = jnp.float32
_BF16 = jnp.bfloat16


def _rmsnorm(x, g):
    return x * lax.rsqrt(jnp.mean(x * x, axis=-1, keepdims=True) + EPS) * g


def _silu(z):
    return z * jax.nn.sigmoid(z)


def _shift_rows(cur, prev_tail, k):
    rolled = pltpu.roll(cur, k, axis=0)
    head = pltpu.roll(prev_tail, k, axis=0)
    row = lax.broadcasted_iota(jnp.int32, head.shape, 0)
    first = jnp.where(row < k, head, rolled[0:V7X_SUBLANES])
    return jnp.concatenate([first, rolled[V7X_SUBLANES:]], axis=0)


def _layer_kernel(x_ref, p_ref, norm_g_ref, w_in_ref, ln_g_ref, ln_b_ref, ws_ref, bias_ref,
                  conv_w_ref, w_out_ref, ple_g_ref, w_gate_ref, w_proj_ref, final_g_ref,
                  o_ref, xc_ref, *, tm, sub, is_last):
    @pl.when(pl.program_id(1) == 0)
    def _():
        xc_ref[...] = jnp.zeros((V7X_SUBLANES, WIDTH_B), _F32)

    tril = (lax.broadcasted_iota(jnp.int32, (CHUNK, CHUNK), 0)
            >= lax.broadcasted_iota(jnp.int32, (CHUNK, CHUNK), 1))
    ws = [jnp.where(tril, ws_ref[h], 0.0).astype(_BF16) for h in range(HEADS_A)]

    n_sub = tm // sub
    st = [{} for _ in range(n_sub)]

    def norm_in(j):
        s = st[j]
        s["x"] = x_ref[0, j * sub:(j + 1) * sub, :]
        s["hn"] = _rmsnorm(s["x"], norm_g_ref[...]).astype(_BF16)

    def project(j):
        s = st[j]

        def proj(off, width):
            return jnp.dot(s["hn"], w_in_ref[:, off:off + width],
                           preferred_element_type=_F32)

        v = proj(GROUPED_OFF_V, WIDTH_A)
        mu = jnp.mean(v, axis=-1, keepdims=True)
        vc = v - mu
        var = jnp.mean(vc * vc, axis=-1, keepdims=True)
        s["v_ln"] = (vc * lax.rsqrt(var + EPS) * ln_g_ref[...] + ln_b_ref[...]).astype(_BF16)
        prev_tail = xc_ref[...] if j == 0 else st[j - 1]["xc_tail"]
        tails, outs = [], []
        for c in range(WIDTH_B // CHUNK):
            lanes = slice(c * CHUNK, (c + 1) * CHUNK)
            r = proj(GROUPED_OFF_B + 4 * CHUNK * c, 4 * CHUNK)
            xc = r[:, CHUNK:2 * CHUNK] * r[:, 0:CHUNK]
            tails.append(xc[sub - V7X_SUBLANES:, :])
            tail_c = prev_tail[:, lanes]
            y = _shift_rows(xc, tail_c, 2) * conv_w_ref[0:1, lanes]
            y = y + _shift_rows(xc, tail_c, 1) * conv_w_ref[1:2, lanes]
            y = y + xc * conv_w_ref[2:3, lanes]
            out_c = r[:, 2 * CHUNK:3 * CHUNK] * y * _silu(r[:, 3 * CHUNK:])
            outs.append(out_c.astype(_BF16))
        s["xc_tail"] = jnp.concatenate(tails, axis=1)
        s["out_b"] = jnp.concatenate(outs, axis=1)
        gates = []
        for pair in range(HEADS_A // 2):
            r = proj(GROUPED_OFF_A + 4 * HEAD_DIM_A * pair, 4 * HEAD_DIM_A)
            gates.append(r[:, 0:HEAD_DIM_A] * _silu(r[:, HEAD_DIM_A:2 * HEAD_DIM_A]))
            gates.append(r[:, 2 * HEAD_DIM_A:3 * HEAD_DIM_A] * _silu(r[:, 3 * HEAD_DIM_A:]))
        s["gate_a"] = jnp.concatenate(gates, axis=1)

    def spatial(j):
        s = st[j]
        n_chunk = sub // CHUNK
        heads = []
        for h in range(HEADS_A):
            cols = slice(h * HEAD_DIM_A, (h + 1) * HEAD_DIM_A)
            v_h = jnp.concatenate(
                [s["v_ln"][c * CHUNK:(c + 1) * CHUNK, cols] for c in range(n_chunk)], axis=1)
            mixed = jnp.dot(ws[h], v_h, preferred_element_type=_F32)
            mixed = jnp.concatenate(
                [mixed[:, c * HEAD_DIM_A:(c + 1) * HEAD_DIM_A] for c in range(n_chunk)], axis=0)
            mixed = mixed + jnp.concatenate([bias_ref[:, cols]] * n_chunk, axis=0)
            heads.append((s["gate_a"][:, cols] * mixed).astype(_BF16))
        s["out_a"] = jnp.concatenate(heads, axis=1)

    def out_project(j):
        s = st[j]
        act = jnp.concatenate([s["out_a"], s["out_b"]], axis=1)
        s["x1"] = s["x"] + jnp.dot(act, w_out_ref[...], preferred_element_type=_F32)
        s["gn"] = _rmsnorm(s["x1"], ple_g_ref[...]).astype(_BF16)

    def embed_project(j):
        r0 = j * sub
        st[j]["pe"] = jnp.dot(p_ref[0, 0, r0:r0 + sub, :].astype(_BF16), w_proj_ref[...],
                              preferred_element_type=_F32)

    def embed_gate(j):
        s, r0 = st[j], j * sub
        gate = jax.nn.sigmoid(jnp.dot(s["gn"], w_gate_ref[...], preferred_element_type=_F32))
        x2 = s["x1"] + gate * s["pe"]
        if is_last:
            x2 = _rmsnorm(x2, final_g_ref[...])
        o_ref[0, r0:r0 + sub, :] = x2

    n_lead = min(2, n_sub)
    norm_in(0)
    for j in range(n_lead):
        embed_project(j)
    project(0)
    for k in range(1, n_sub):
        norm_in(k)
        spatial(k - 1)
        project(k)
        out_project(k - 1)
        if k >= 2:
            embed_gate(k - 2)
    spatial(n_sub - 1)
    for j in range(n_lead, n_sub):
        embed_project(j)
    out_project(n_sub - 1)
    if n_sub >= 2:
        embed_gate(n_sub - 2)
    embed_gate(n_sub - 1)

    xc_ref[...] = st[n_sub - 1]["xc_tail"]


def _group_columns(w_in):
    def cols(off, i, width):
        return w_in[:, :, off + i * width:off + (i + 1) * width]

    parts = [w_in[:, :, OFF_V:OFF_V + WIDTH_A]]
    for h in range(HEADS_A):
        parts += [cols(OFF_U, h, HEAD_DIM_A), cols(OFF_ZA, h, HEAD_DIM_A)]
    for c in range(WIDTH_B // CHUNK):
        parts += [cols(OFF_H, c, CHUNK), cols(OFF_GC, c, CHUNK), cols(OFF_GB, c, CHUNK),
                  cols(OFF_ZB, c, CHUNK)]
    return jnp.concatenate(parts, axis=2)


def _layer_call(x, p, layer, weights, final_g, *, is_last):
    bsz, seq, _ = x.shape
    tm = SEQ_TILE
    assert seq % tm == 0 and tm % SUB_TILE == 0 and SUB_TILE % CHUNK == 0

    def resident(shape):
        return pl.BlockSpec((None,) + shape, lambda b, s: (layer,) + (0,) * len(shape),
                            pipeline_mode=pl.Buffered(1))

    in_specs = [
        pl.BlockSpec((1, tm, D_MODEL), lambda b, s: (b, s, 0)),
        pl.BlockSpec((1, 1, tm, PLE_DIM), lambda b, s: (layer, b, s, 0)),
        resident((1, D_MODEL)),
        resident((D_MODEL, PROJ_WIDTH)),
        resident((1, WIDTH_A)),
        resident((1, WIDTH_A)),
        resident((HEADS_A, CHUNK, CHUNK)),
        resident((CHUNK, WIDTH_A)),
        resident((CONV_K, WIDTH_B)),
        resident((D_MODEL, D_MODEL)),
        resident((1, D_MODEL)),
        resident((D_MODEL, D_MODEL)),
        resident((PLE_DIM, D_MODEL)),
        pl.BlockSpec((1, D_MODEL), lambda b, s: (0, 0), pipeline_mode=pl.Buffered(1)),
    ]
    return pl.pallas_call(
        functools.partial(_layer_kernel, tm=tm, sub=SUB_TILE, is_last=is_last),
        grid=(bsz, seq // tm),
        in_specs=in_specs,
        out_specs=pl.BlockSpec((1, tm, D_MODEL), lambda b, s: (b, s, 0)),
        out_shape=jax.ShapeDtypeStruct(x.shape, x.dtype),
        scratch_shapes=[
            pltpu.VMEM((V7X_SUBLANES, WIDTH_B), _F32),
        ],
        compiler_params=pltpu.CompilerParams(
            dimension_semantics=("arbitrary", "arbitrary"),
            vmem_limit_bytes=V7X_VMEM_LIMIT_BYTES),
        name=f"trunk_layer{layer}",
    )(x, p, *weights, final_g)


def kernel(x, p, norm_g, w_in, ln_v_g, ln_v_b, w_s, b_s, conv_w, w_out, ple_norm_g,
           w_ple_gate, w_ple_proj, final_g):
    depth = w_in.shape[0]
    bias = jnp.repeat(jnp.swapaxes(b_s, 1, 2), HEAD_DIM_A, axis=2)
    weights = (
        norm_g[:, None, :], _group_columns(w_in).astype(_BF16), ln_v_g[:, None, :],
        ln_v_b[:, None, :], w_s, bias, jnp.swapaxes(conv_w, 1, 2), w_out.astype(_BF16),
        ple_norm_g[:, None, :], w_ple_gate.astype(_BF16), w_ple_proj.astype(_BF16))
    for layer in range(depth):
        x = _layer_call(x, p, layer, weights, final_g[None, :], is_last=layer == depth - 1)
    return x
```

```python
import functools

import jax
import jax.numpy as jnp
from jax import lax
from jax.experimental import pallas as pl
from jax.experimental.pallas import tpu as pltpu

D_MODEL = 1024
WIDTH_A = 512
WIDTH_B = 512
HEADS_A = 4
HEAD_DIM_A = WIDTH_A // HEADS_A
CHUNK = 128
CONV_K = 3
PLE_DIM = 256
EPS = 1e-6
OFF_U, OFF_V, OFF_ZA, OFF_H, OFF_GB, OFF_GC, OFF_ZB = (
    0, WIDTH_A, 2 * WIDTH_A, 3 * WIDTH_A, 3 * WIDTH_A + WIDTH_B,
    3 * WIDTH_A + 2 * WIDTH_B, 3 * WIDTH_A + 3 * WIDTH_B)
PROJ_WIDTH = 3 * WIDTH_A + 4 * WIDTH_B
GROUPED_OFF_V = 0
GROUPED_OFF_A = WIDTH_A
GROUPED_OFF_B = 3 * WIDTH_A

V7X_SUBLANES = 8
V7X_VMEM_LIMIT_BYTES = 56 * 1024 * 1024
SEQ_TILE = 1024
SUB_TILE = 256

_F32 = jnp.float32
_BF16 = jnp.bfloat16


def _rmsnorm(x, g):
    return x * lax.rsqrt(jnp.mean(x * x, axis=-1, keepdims=True) + EPS) * g


def _silu(z):
    return z * jax.nn.sigmoid(z)


def _shift_rows(cur, prev_tail, k):
    rolled = pltpu.roll(cur, k, axis=0)
    head = pltpu.roll(prev_tail, k, axis=0)
    row = lax.broadcasted_iota(jnp.int32, head.shape, 0)
    first = jnp.where(row < k, head, rolled[0:V7X_SUBLANES])
    return jnp.concatenate([first, rolled[V7X_SUBLANES:]], axis=0)


def _layer_kernel(x_ref, p_ref, norm_g_ref, w_in_ref, ln_g_ref, ln_b_ref, ws_ref, bias_ref,
                  conv_w_ref, w_out_ref, ple_g_ref, w_gate_ref, w_proj_ref, final_g_ref,
                  o_ref, xc_ref, *, tm, sub, layer, is_last):
    @pl.when(pl.program_id(1) == 0)
    def _():
        xc_ref[...] = jnp.zeros((V7X_SUBLANES, WIDTH_B), _F32)

    tril = (lax.broadcasted_iota(jnp.int32, (CHUNK, CHUNK), 0)
            >= lax.broadcasted_iota(jnp.int32, (CHUNK, CHUNK), 1))
    ws = [jnp.where(tril, ws_ref[h], 0.0).astype(_BF16) for h in range(HEADS_A)]

    n_sub = tm // sub
    st = [{} for _ in range(n_sub)]

    def norm_in(j):
        s = st[j]
        s["x"] = x_ref[0, j * sub:(j + 1) * sub, :]
        s["hn"] = _rmsnorm(s["x"], norm_g_ref[layer:layer + 1, :]).astype(_BF16)

    def project(j):
        s = st[j]

        def proj(off, width):
            return jnp.dot(s["hn"], w_in_ref[:, off:off + width],
                           preferred_element_type=_F32)

        v = proj(GROUPED_OFF_V, WIDTH_A)
        mu = jnp.mean(v, axis=-1, keepdims=True)
        vc = v - mu
        var = jnp.mean(vc * vc, axis=-1, keepdims=True)
        v_ln = vc * lax.rsqrt(var + EPS) * ln_g_ref[layer:layer + 1, :]
        s["v_ln"] = (v_ln + ln_b_ref[layer:layer + 1, :]).astype(_BF16)
        prev_tail = xc_ref[...] if j == 0 else st[j - 1]["xc_tail"]
        tails, outs = [], []
        for c in range(WIDTH_B // CHUNK):
            lanes = slice(c * CHUNK, (c + 1) * CHUNK)
            r = proj(GROUPED_OFF_B + 4 * CHUNK * c, 4 * CHUNK)
            xc = r[:, 2 * CHUNK:3 * CHUNK] * r[:, 0:CHUNK]
            tails.append(xc[sub - V7X_SUBLANES:, :])
            tail_c = prev_tail[:, lanes]
            y = _shift_rows(xc, tail_c, 2) * conv_w_ref[0:1, lanes]
            y = y + _shift_rows(xc, tail_c, 1) * conv_w_ref[1:2, lanes]
            y = y + xc * conv_w_ref[2:3, lanes]
            out_c = r[:, CHUNK:2 * CHUNK] * y * _silu(r[:, 3 * CHUNK:])
            outs.append(out_c.astype(_BF16))
        s["xc_tail"] = jnp.concatenate(tails, axis=1)
        s["out_b"] = jnp.concatenate(outs, axis=1)
        gates = []
        for pair in range(HEADS_A // 2):
            r = proj(GROUPED_OFF_A + 4 * HEAD_DIM_A * pair, 4 * HEAD_DIM_A)
            gates.append(r[:, 0:HEAD_DIM_A] * _silu(r[:, HEAD_DIM_A:2 * HEAD_DIM_A]))
            gates.append(r[:, 2 * HEAD_DIM_A:3 * HEAD_DIM_A] * _silu(r[:, 3 * HEAD_DIM_A:]))
        s["gate_a"] = jnp.concatenate(gates, axis=1)

    def spatial(j):
        s = st[j]
        n_chunk = sub // CHUNK
        heads = []
        for h in range(HEADS_A):
            cols = slice(h * HEAD_DIM_A, (h + 1) * HEAD_DIM_A)
            v_h = jnp.concatenate(
                [s["v_ln"][c * CHUNK:(c + 1) * CHUNK, cols] for c in range(n_chunk)], axis=1)
            mixed = jnp.dot(ws[h], v_h, preferred_element_type=_F32)
            mixed = jnp.concatenate(
                [mixed[:, c * HEAD_DIM_A:(c + 1) * HEAD_DIM_A] for c in range(n_chunk)], axis=0)
            mixed = mixed + jnp.concatenate([bias_ref[:, cols]] * n_chunk, axis=0)
            heads.append((s["gate_a"][:, cols] * mixed).astype(_BF16))
        s["out_a"] = jnp.concatenate(heads, axis=1)

    def out_project(j):
        s = st[j]
        act = jnp.concatenate([s["out_a"], s["out_b"]], axis=1)
        s["x1"] = s["x"] + jnp.dot(act, w_out_ref[...], preferred_element_type=_F32)
        s["gn"] = _rmsnorm(s["x1"], ple_g_ref[layer:layer + 1, :]).astype(_BF16)

    def embed_project(j):
        r0 = j * sub
        st[j]["pe"] = jnp.dot(p_ref[0, 0, r0:r0 + sub, :].astype(_BF16), w_proj_ref[...],
                              preferred_element_type=_F32)

    def embed_gate(j):
        s, r0 = st[j], j * sub
        gate = jax.nn.sigmoid(jnp.dot(s["gn"], w_gate_ref[...], preferred_element_type=_F32))
        x2 = s["x1"] + gate * s["pe"]
        if is_last:
            x2 = _rmsnorm(x2, final_g_ref[...])
        o_ref[0, r0:r0 + sub, :] = x2

    n_lead = min(2, n_sub)
    norm_in(0)
    for j in range(n_lead):
        embed_project(j)
    project(0)
    for k in range(1, n_sub):
        norm_in(k)
        spatial(k - 1)
        project(k)
        out_project(k - 1)
        if k >= 2:
            embed_gate(k - 2)
    spatial(n_sub - 1)
    for j in range(n_lead, n_sub):
        embed_project(j)
    out_project(n_sub - 1)
    if n_sub >= 2:
        embed_gate(n_sub - 2)
    embed_gate(n_sub - 1)

    xc_ref[...] = st[n_sub - 1]["xc_tail"]


def _group_columns(w_in):
    depth = w_in.shape[0]

    def interleave(block, n_groups, width):
        b = block.reshape(depth, D_MODEL, n_groups, -1, width)
        return jnp.swapaxes(b, 2, 3).reshape(depth, D_MODEL, -1)

    v = w_in[:, :, OFF_V:OFF_V + WIDTH_A]
    u_z = jnp.concatenate(
        [w_in[:, :, OFF_U:OFF_U + WIDTH_A], w_in[:, :, OFF_ZA:OFF_ZA + WIDTH_A]], axis=2)
    group_b = w_in[:, :, OFF_H:]
    return jnp.concatenate(
        [v, interleave(u_z, 2, HEAD_DIM_A), interleave(group_b, 4, CHUNK)], axis=2)


def _layer_call(x, p, layer, weights, final_g, *, is_last):
    bsz, seq, _ = x.shape
    depth = weights[0].shape[0]
    tm = SEQ_TILE
    assert seq % tm == 0 and tm % SUB_TILE == 0 and SUB_TILE % CHUNK == 0

    def resident(shape):
        return pl.BlockSpec((None,) + shape, lambda b, s: (layer,) + (0,) * len(shape),
                            pipeline_mode=pl.Buffered(1))

    def whole(shape):
        return pl.BlockSpec(shape, lambda b, s: (0,) * len(shape),
                            pipeline_mode=pl.Buffered(1))

    in_specs = [
        pl.BlockSpec((1, tm, D_MODEL), lambda b, s: (b, s, 0)),
        pl.BlockSpec((1, 1, tm, PLE_DIM), lambda b, s: (layer, b, s, 0)),
        whole((depth, D_MODEL)),
        resident((D_MODEL, PROJ_WIDTH)),
        whole((depth, WIDTH_A)),
        whole((depth, WIDTH_A)),
        resident((HEADS_A, CHUNK, CHUNK)),
        resident((CHUNK, WIDTH_A)),
        resident((CONV_K, WIDTH_B)),
        resident((D_MODEL, D_MODEL)),
        whole((depth, D_MODEL)),
        resident((D_MODEL, D_MODEL)),
        resident((PLE_DIM, D_MODEL)),
        whole((1, D_MODEL)),
    ]
    return pl.pallas_call(
        functools.partial(_layer_kernel, tm=tm, sub=SUB_TILE, layer=layer, is_last=is_last),
        grid=(bsz, seq // tm),
        in_specs=in_specs,
        out_specs=pl.BlockSpec((1, tm, D_MODEL), lambda b, s: (b, s, 0)),
        out_shape=jax.ShapeDtypeStruct(x.shape, x.dtype),
        scratch_shapes=[
            pltpu.VMEM((V7X_SUBLANES, WIDTH_B), _F32),
        ],
        compiler_params=pltpu.CompilerParams(
            dimension_semantics=("arbitrary", "arbitrary"),
            vmem_limit_bytes=V7X_VMEM_LIMIT_BYTES),
        name=f"trunk_layer{layer}",
    )(x, p, *weights, final_g)


def kernel(x, p, norm_g, w_in, ln_v_g, ln_v_b, w_s, b_s, conv_w, w_out, ple_norm_g,
           w_ple_gate, w_ple_proj, final_g):
    depth = w_in.shape[0]
    bias = jnp.repeat(jnp.swapaxes(b_s, 1, 2), HEAD_DIM_A, axis=2)
    weights = (
        norm_g, _group_columns(w_in.astype(_BF16)), ln_v_g, ln_v_b, w_s, bias,
        jnp.swapaxes(conv_w, 1, 2), w_out.astype(_BF16), ple_norm_g,
        w_ple_gate.astype(_BF16), w_ple_proj.astype(_BF16))
    for layer in range(depth):
        x = _layer_call(x, p, layer, weights, final_g[None, :], is_last=layer == depth - 1)
    return x
```

```python
import functools

import jax
import jax.numpy as jnp
from jax import lax
from jax.experimental import pallas as pl
from jax.experimental.pallas import tpu as pltpu

D_MODEL = 1024
WIDTH_A = 512
WIDTH_B = 512
HEADS_A = 4
HEAD_DIM_A = WIDTH_A // HEADS_A
CHUNK = 128
CONV_K = 3
PLE_DIM = 256
EPS = 1e-6
OFF_U, OFF_V, OFF_ZA, OFF_H, OFF_GB, OFF_GC, OFF_ZB = (
    0, WIDTH_A, 2 * WIDTH_A, 3 * WIDTH_A, 3 * WIDTH_A + WIDTH_B,
    3 * WIDTH_A + 2 * WIDTH_B, 3 * WIDTH_A + 3 * WIDTH_B)
PROJ_WIDTH = 3 * WIDTH_A + 4 * WIDTH_B
GROUPED_OFF_V = 0
GROUPED_OFF_A = WIDTH_A
GROUPED_OFF_B = 3 * WIDTH_A

V7X_SUBLANES = 8
V7X_VMEM_LIMIT_BYTES = 56 * 1024 * 1024
SEQ_TILE = 1024
SUB_TILE = 256
PREP_ROW_BLOCKS = 4

_F32 = jnp.float32
_BF16 = jnp.bfloat16


def _rmsnorm(x, g):
    return x * lax.rsqrt(jnp.mean(x * x, axis=-1, keepdims=True) + EPS) * g


def _silu(z):
    return z * jax.nn.sigmoid(z)


def _shift_rows(cur, prev_tail, k):
    rolled = pltpu.roll(cur, k, axis=0)
    head = pltpu.roll(prev_tail, k, axis=0)
    row = lax.broadcasted_iota(jnp.int32, head.shape, 0)
    first = jnp.where(row < k, head, rolled[0:V7X_SUBLANES])
    return jnp.concatenate([first, rolled[V7X_SUBLANES:]], axis=0)


def _layer_kernel(x_ref, p_ref, norm_g_ref, w_in_ref, ln_g_ref, ln_b_ref, ws_ref, bias_ref,
                  conv_w_ref, w_out_ref, ple_g_ref, w_gate_ref, w_proj_ref, final_g_ref,
                  o_ref, xc_ref, *, tm, sub, layer, is_last):
    @pl.when(pl.program_id(1) == 0)
    def _():
        xc_ref[...] = jnp.zeros((V7X_SUBLANES, WIDTH_B), _F32)

    tril = (lax.broadcasted_iota(jnp.int32, (CHUNK, CHUNK), 0)
            >= lax.broadcasted_iota(jnp.int32, (CHUNK, CHUNK), 1))
    ws = [jnp.where(tril, ws_ref[h], 0.0).astype(_BF16) for h in range(HEADS_A)]

    n_sub = tm // sub
    st = [{} for _ in range(n_sub)]

    def norm_in(j):
        s = st[j]
        s["x"] = x_ref[0, j * sub:(j + 1) * sub, :]
        s["hn"] = _rmsnorm(s["x"], norm_g_ref[layer:layer + 1, :]).astype(_BF16)

    def project(j):
        s = st[j]

        def proj(off, width):
            return jnp.dot(s["hn"], w_in_ref[:, off:off + width],
                           preferred_element_type=_F32)

        v = proj(GROUPED_OFF_V, WIDTH_A)
        mu = jnp.mean(v, axis=-1, keepdims=True)
        vc = v - mu
        var = jnp.mean(vc * vc, axis=-1, keepdims=True)
        v_ln = vc * lax.rsqrt(var + EPS) * ln_g_ref[layer:layer + 1, :]
        s["v_ln"] = (v_ln + ln_b_ref[layer:layer + 1, :]).astype(_BF16)
        prev_tail = xc_ref[...] if j == 0 else st[j - 1]["xc_tail"]
        tails, outs = [], []
        for c in range(WIDTH_B // CHUNK):
            lanes = slice(c * CHUNK, (c + 1) * CHUNK)
            r = proj(GROUPED_OFF_B + 4 * CHUNK * c, 4 * CHUNK)
            xc = r[:, 2 * CHUNK:3 * CHUNK] * r[:, 0:CHUNK]
            tails.append(xc[sub - V7X_SUBLANES:, :])
            tail_c = prev_tail[:, lanes]
            y = _shift_rows(xc, tail_c, 2) * conv_w_ref[0:1, lanes]
            y = y + _shift_rows(xc, tail_c, 1) * conv_w_ref[1:2, lanes]
            y = y + xc * conv_w_ref[2:3, lanes]
            out_c = r[:, CHUNK:2 * CHUNK] * y * _silu(r[:, 3 * CHUNK:])
            outs.append(out_c.astype(_BF16))
        s["xc_tail"] = jnp.concatenate(tails, axis=1)
        s["out_b"] = jnp.concatenate(outs, axis=1)
        gates = []
        for pair in range(HEADS_A // 2):
            r = proj(GROUPED_OFF_A + 4 * HEAD_DIM_A * pair, 4 * HEAD_DIM_A)
            gates.append(r[:, 0:HEAD_DIM_A] * _silu(r[:, HEAD_DIM_A:2 * HEAD_DIM_A]))
            gates.append(r[:, 2 * HEAD_DIM_A:3 * HEAD_DIM_A] * _silu(r[:, 3 * HEAD_DIM_A:]))
        s["gate_a"] = jnp.concatenate(gates, axis=1)

    def spatial(j):
        s = st[j]
        n_chunk = sub // CHUNK
        heads = []
        for h in range(HEADS_A):
            cols = slice(h * HEAD_DIM_A, (h + 1) * HEAD_DIM_A)
            v_h = jnp.concatenate(
                [s["v_ln"][c * CHUNK:(c + 1) * CHUNK, cols] for c in range(n_chunk)], axis=1)
            mixed = jnp.dot(ws[h], v_h, preferred_element_type=_F32)
            mixed = jnp.concatenate(
                [mixed[:, c * HEAD_DIM_A:(c + 1) * HEAD_DIM_A] for c in range(n_chunk)], axis=0)
            mixed = mixed + jnp.concatenate([bias_ref[:, cols]] * n_chunk, axis=0)
            heads.append((s["gate_a"][:, cols] * mixed).astype(_BF16))
        s["out_a"] = jnp.concatenate(heads, axis=1)

    def out_project(j):
        s = st[j]
        act = jnp.concatenate([s["out_a"], s["out_b"]], axis=1)
        s["x1"] = s["x"] + jnp.dot(act, w_out_ref[...], preferred_element_type=_F32)
        s["gn"] = _rmsnorm(s["x1"], ple_g_ref[layer:layer + 1, :]).astype(_BF16)

    def embed_project(j):
        r0 = j * sub
        st[j]["pe"] = jnp.dot(p_ref[0, 0, r0:r0 + sub, :].astype(_BF16), w_proj_ref[...],
                              preferred_element_type=_F32)

    def embed_gate(j):
        s, r0 = st[j], j * sub
        gate = jax.nn.sigmoid(jnp.dot(s["gn"], w_gate_ref[...], preferred_element_type=_F32))
        x2 = s["x1"] + gate * s["pe"]
        if is_last:
            x2 = _rmsnorm(x2, final_g_ref[...])
        o_ref[0, r0:r0 + sub, :] = x2

    n_lead = min(2, n_sub)
    norm_in(0)
    for j in range(n_lead):
        embed_project(j)
    project(0)
    for k in range(1, n_sub):
        norm_in(k)
        spatial(k - 1)
        project(k)
        out_project(k - 1)
        if k >= 2:
            embed_gate(k - 2)
    spatial(n_sub - 1)
    for j in range(n_lead, n_sub):
        embed_project(j)
    out_project(n_sub - 1)
    if n_sub >= 2:
        embed_gate(n_sub - 2)
    embed_gate(n_sub - 1)

    xc_ref[...] = st[n_sub - 1]["xc_tail"]


def _grouped_sources():
    src = [OFF_V + i * CHUNK for i in range(WIDTH_A // CHUNK)]
    for h in range(HEADS_A):
        src += [OFF_U + h * HEAD_DIM_A, OFF_ZA + h * HEAD_DIM_A]
    for c in range(WIDTH_B // CHUNK):
        src += [OFF_H + c * CHUNK, OFF_GB + c * CHUNK, OFF_GC + c * CHUNK, OFF_ZB + c * CHUNK]
    return src


def _prep_kernel(w_in_ref, w_out_ref, w_gate_ref, w_proj_ref,
                 w_in_o, w_out_o, w_gate_o, w_proj_o):
    for piece, src in enumerate(_grouped_sources()):
        w_in_o[:, piece * CHUNK:(piece + 1) * CHUNK] = (
            w_in_ref[:, src:src + CHUNK].astype(_BF16))
    w_out_o[...] = w_out_ref[...].astype(_BF16)
    w_gate_o[...] = w_gate_ref[...].astype(_BF16)
    w_proj_o[...] = w_proj_ref[...].astype(_BF16)


def _prep_weights(w_in, w_out, w_gate, w_proj):
    depth = w_in.shape[0]
    n_blk = PREP_ROW_BLOCKS
    arrays = (w_in, w_out, w_gate, w_proj)

    def spec(a):
        return pl.BlockSpec((None, a.shape[1] // n_blk, a.shape[2]), lambda d, r: (d, r, 0))

    return pl.pallas_call(
        _prep_kernel,
        grid=(depth, n_blk),
        in_specs=[spec(a) for a in arrays],
        out_specs=[spec(a) for a in arrays],
        out_shape=[jax.ShapeDtypeStruct(a.shape, _BF16) for a in arrays],
        compiler_params=pltpu.CompilerParams(
            dimension_semantics=("arbitrary", "arbitrary"),
            vmem_limit_bytes=V7X_VMEM_LIMIT_BYTES),
        name="trunk_weight_prep",
    )(*arrays)


def _layer_call(x, p, layer, weights, final_g, *, is_last):
    bsz, seq, _ = x.shape
    depth = weights[0].shape[0]
    tm = SEQ_TILE
    assert seq % tm == 0 and tm % SUB_TILE == 0 and SUB_TILE % CHUNK == 0

    def resident(shape):
        return pl.BlockSpec((None,) + shape, lambda b, s: (layer,) + (0,) * len(shape),
                            pipeline_mode=pl.Buffered(1))

    def whole(shape):
        return pl.BlockSpec(shape, lambda b, s: (0,) * len(shape),
                            pipeline_mode=pl.Buffered(1))

    in_specs = [
        pl.BlockSpec((1, tm, D_MODEL), lambda b, s: (b, s, 0)),
        pl.BlockSpec((1, 1, tm, PLE_DIM), lambda b, s: (layer, b, s, 0)),
        whole((depth, D_MODEL)),
        resident((D_MODEL, PROJ_WIDTH)),
        whole((depth, WIDTH_A)),
        whole((depth, WIDTH_A)),
        resident((HEADS_A, CHUNK, CHUNK)),
        resident((CHUNK, WIDTH_A)),
        resident((CONV_K, WIDTH_B)),
        resident((D_MODEL, D_MODEL)),
        whole((depth, D_MODEL)),
        resident((D_MODEL, D_MODEL)),
        resident((PLE_DIM, D_MODEL)),
        whole((1, D_MODEL)),
    ]
    return pl.pallas_call(
        functools.partial(_layer_kernel, tm=tm, sub=SUB_TILE, layer=layer, is_last=is_last),
        grid=(bsz, seq // tm),
        in_specs=in_specs,
        out_specs=pl.BlockSpec((1, tm, D_MODEL), lambda b, s: (b, s, 0)),
        out_shape=jax.ShapeDtypeStruct(x.shape, x.dtype),
        scratch_shapes=[
            pltpu.VMEM((V7X_SUBLANES, WIDTH_B), _F32),
        ],
        compiler_params=pltpu.CompilerParams(
            dimension_semantics=("arbitrary", "arbitrary"),
            vmem_limit_bytes=V7X_VMEM_LIMIT_BYTES),
        name=f"trunk_layer{layer}",
    )(x, p, *weights, final_g)


def kernel(x, p, norm_g, w_in, ln_v_g, ln_v_b, w_s, b_s, conv_w, w_out, ple_norm_g,
           w_ple_gate, w_ple_proj, final_g):
    depth = w_in.shape[0]
    bias = jnp.repeat(jnp.swapaxes(b_s, 1, 2), HEAD_DIM_A, axis=2)
    w_in_g, w_out_b, w_gate_b, w_proj_b = _prep_weights(w_in, w_out, w_ple_gate, w_ple_proj)
    weights = (norm_g, w_in_g, ln_v_g, ln_v_b, w_s, bias, jnp.swapaxes(conv_w, 1, 2),
               w_out_b, ple_norm_g, w_gate_b, w_proj_b)
    for layer in range(depth):
        x = _layer_call(x, p, layer, weights, final_g[None, :], is_last=layer == depth - 1)
    return x
```

```python
import functools

import jax
import jax.numpy as jnp
from jax import lax
from jax.experimental import pallas as pl
from jax.experimental.pallas import tpu as pltpu

D_MODEL = 1024
WIDTH_A = 512
WIDTH_B = 512
HEADS_A = 4
HEAD_DIM_A = WIDTH_A // HEADS_A
CHUNK = 128
CONV_K = 3
PLE_DIM = 256
EPS = 1e-6
OFF_U, OFF_V, OFF_ZA, OFF_H, OFF_GB, OFF_GC, OFF_ZB = (
    0, WIDTH_A, 2 * WIDTH_A, 3 * WIDTH_A, 3 * WIDTH_A + WIDTH_B,
    3 * WIDTH_A + 2 * WIDTH_B, 3 * WIDTH_A + 3 * WIDTH_B)
PROJ_WIDTH = 3 * WIDTH_A + 4 * WIDTH_B
GROUPED_OFF_V = 0
GROUPED_OFF_A = WIDTH_A
GROUPED_OFF_B = 3 * WIDTH_A

V7X_SUBLANES = 8
V7X_VMEM_LIMIT_BYTES = 56 * 1024 * 1024
SEQ_TILE = 1024
SUB_TILE = 256
PREP_ROW_BLOCKS = 4

_F32 = jnp.float32
_BF16 = jnp.bfloat16


def _rmsnorm(x, g):
    return x * lax.rsqrt(jnp.mean(x * x, axis=-1, keepdims=True) + EPS) * g


def _silu(z):
    return z * jax.nn.sigmoid(z)


def _shift_rows(cur, prev_tail, k):
    rolled = pltpu.roll(cur, k, axis=0)
    head = pltpu.roll(prev_tail, k, axis=0)
    row = lax.broadcasted_iota(jnp.int32, head.shape, 0)
    first = jnp.where(row < k, head, rolled[0:V7X_SUBLANES])
    return jnp.concatenate([first, rolled[V7X_SUBLANES:]], axis=0)


def _layer_kernel(x_ref, p_ref, norm_g_ref, w_in_ref, ln_g_ref, ln_b_ref, ws_ref, bias_ref,
                  conv_w_ref, w_out_ref, ple_g_ref, w_gate_ref, w_proj_ref, final_g_ref,
                  o_ref, xc_ref, *, tm, sub, layer, is_last):
    @pl.when(pl.program_id(1) == 0)
    def _():
        xc_ref[...] = jnp.zeros((V7X_SUBLANES, WIDTH_B), _F32)

    tril = (lax.broadcasted_iota(jnp.int32, (CHUNK, CHUNK), 0)
            >= lax.broadcasted_iota(jnp.int32, (CHUNK, CHUNK), 1))
    ws = [jnp.where(tril, ws_ref[h], 0.0).astype(_BF16) for h in range(HEADS_A)]

    n_sub = tm // sub
    st = [{} for _ in range(n_sub)]

    def norm_in(j):
        s = st[j]
        s["x"] = x_ref[0, j * sub:(j + 1) * sub, :]
        s["hn"] = _rmsnorm(s["x"], norm_g_ref[layer:layer + 1, :]).astype(_BF16)

    def project(j):
        s = st[j]

        def proj(off, width):
            return jnp.dot(s["hn"], w_in_ref[:, off:off + width],
                           preferred_element_type=_F32)

        v = proj(GROUPED_OFF_V, WIDTH_A)
        mu = jnp.mean(v, axis=-1, keepdims=True)
        vc = v - mu
        var = jnp.mean(vc * vc, axis=-1, keepdims=True)
        v_ln = vc * lax.rsqrt(var + EPS) * ln_g_ref[layer:layer + 1, :]
        s["v_ln"] = (v_ln + ln_b_ref[layer:layer + 1, :]).astype(_BF16)
        prev_tail = xc_ref[...] if j == 0 else st[j - 1]["xc_tail"]
        tails, outs = [], []
        for c in range(WIDTH_B // CHUNK):
            lanes = slice(c * CHUNK, (c + 1) * CHUNK)
            r = proj(GROUPED_OFF_B + 4 * CHUNK * c, 4 * CHUNK)
            xc = r[:, 2 * CHUNK:3 * CHUNK] * r[:, 0:CHUNK]
            tails.append(xc[sub - V7X_SUBLANES:, :])
            tail_c = prev_tail[:, lanes]
            y = _shift_rows(xc, tail_c, 2) * conv_w_ref[0:1, lanes]
            y = y + _shift_rows(xc, tail_c, 1) * conv_w_ref[1:2, lanes]
            y = y + xc * conv_w_ref[2:3, lanes]
            out_c = r[:, CHUNK:2 * CHUNK] * y * _silu(r[:, 3 * CHUNK:])
            outs.append(out_c.astype(_BF16))
        s["xc_tail"] = jnp.concatenate(tails, axis=1)
        s["out_b"] = jnp.concatenate(outs, axis=1)
        gates = []
        for pair in range(HEADS_A // 2):
            r = proj(GROUPED_OFF_A + 4 * HEAD_DIM_A * pair, 4 * HEAD_DIM_A)
            gates.append(r[:, 0:HEAD_DIM_A] * _silu(r[:, HEAD_DIM_A:2 * HEAD_DIM_A]))
            gates.append(r[:, 2 * HEAD_DIM_A:3 * HEAD_DIM_A] * _silu(r[:, 3 * HEAD_DIM_A:]))
        s["gate_a"] = jnp.concatenate(gates, axis=1)

    def spatial(j):
        s = st[j]
        n_chunk = sub // CHUNK
        heads = []
        for h in range(HEADS_A):
            cols = slice(h * HEAD_DIM_A, (h + 1) * HEAD_DIM_A)
            v_h = jnp.concatenate(
                [s["v_ln"][c * CHUNK:(c + 1) * CHUNK, cols] for c in range(n_chunk)], axis=1)
            mixed = jnp.dot(ws[h], v_h, preferred_element_type=_F32)
            mixed = jnp.concatenate(
                [mixed[:, c * HEAD_DIM_A:(c + 1) * HEAD_DIM_A] for c in range(n_chunk)], axis=0)
            mixed = mixed + jnp.concatenate([bias_ref[:, cols]] * n_chunk, axis=0)
            heads.append((s["gate_a"][:, cols] * mixed).astype(_BF16))
        s["out_a"] = jnp.concatenate(heads, axis=1)

    def out_project(j):
        s = st[j]
        act = jnp.concatenate([s["out_a"], s["out_b"]], axis=1)
        s["x1"] = s["x"] + jnp.dot(act, w_out_ref[...], preferred_element_type=_F32)
        s["gn"] = _rmsnorm(s["x1"], ple_g_ref[layer:layer + 1, :]).astype(_BF16)

    def embed_project(j):
        r0 = j * sub
        st[j]["pe"] = jnp.dot(p_ref[0, 0, r0:r0 + sub, :].astype(_BF16), w_proj_ref[...],
                              preferred_element_type=_F32)

    def embed_gate(j, project_after=False):
        s, r0 = st[j], j * sub
        gate = jax.nn.sigmoid(jnp.dot(s["gn"], w_gate_ref[...], preferred_element_type=_F32))
        if project_after:
            embed_project(j)
        x2 = s["x1"] + gate * s["pe"]
        if is_last:
            x2 = _rmsnorm(x2, final_g_ref[...])
        o_ref[0, r0:r0 + sub, :] = x2

    n_lead = min(2, n_sub)
    norm_in(0)
    for j in range(n_lead):
        embed_project(j)
    project(0)
    for k in range(1, n_sub):
        norm_in(k)
        spatial(k - 1)
        project(k)
        out_project(k - 1)
        if k >= 2:
            embed_gate(k - 2)
    spatial(n_sub - 1)
    for j in range(n_lead, n_sub - 1):
        embed_project(j)
    out_project(n_sub - 1)
    if n_sub >= 2:
        embed_gate(n_sub - 2)
    embed_gate(n_sub - 1, project_after=n_sub > n_lead)

    xc_ref[...] = st[n_sub - 1]["xc_tail"]


def _grouped_sources():
    src = [OFF_V + i * CHUNK for i in range(WIDTH_A // CHUNK)]
    for h in range(HEADS_A):
        src += [OFF_U + h * HEAD_DIM_A, OFF_ZA + h * HEAD_DIM_A]
    for c in range(WIDTH_B // CHUNK):
        src += [OFF_H + c * CHUNK, OFF_GB + c * CHUNK, OFF_GC + c * CHUNK, OFF_ZB + c * CHUNK]
    return src


def _prep_kernel(w_in_ref, w_out_ref, w_gate_ref, w_proj_ref,
                 w_in_o, w_out_o, w_gate_o, w_proj_o):
    for piece, src in enumerate(_grouped_sources()):
        w_in_o[:, piece * CHUNK:(piece + 1) * CHUNK] = (
            w_in_ref[:, src:src + CHUNK].astype(_BF16))
    w_out_o[...] = w_out_ref[...].astype(_BF16)
    w_gate_o[...] = w_gate_ref[...].astype(_BF16)
    w_proj_o[...] = w_proj_ref[...].astype(_BF16)


def _prep_weights(w_in, w_out, w_gate, w_proj):
    depth = w_in.shape[0]
    n_blk = PREP_ROW_BLOCKS
    arrays = (w_in, w_out, w_gate, w_proj)

    def spec(a):
        return pl.BlockSpec((None, a.shape[1] // n_blk, a.shape[2]), lambda d, r: (d, r, 0))

    return pl.pallas_call(
        _prep_kernel,
        grid=(depth, n_blk),
        in_specs=[spec(a) for a in arrays],
        out_specs=[spec(a) for a in arrays],
        out_shape=[jax.ShapeDtypeStruct(a.shape, _BF16) for a in arrays],
        compiler_params=pltpu.CompilerParams(
            dimension_semantics=("arbitrary", "arbitrary"),
            vmem_limit_bytes=V7X_VMEM_LIMIT_BYTES),
        name="trunk_weight_prep",
    )(*arrays)


def _layer_call(x, p, layer, weights, final_g, *, is_last):
    bsz, seq, _ = x.shape
    depth = weights[0].shape[0]
    tm = SEQ_TILE
    assert seq % tm == 0 and tm % SUB_TILE == 0 and SUB_TILE % CHUNK == 0

    def resident(shape):
        return pl.BlockSpec((None,) + shape, lambda b, s: (layer,) + (0,) * len(shape),
                            pipeline_mode=pl.Buffered(1))

    def whole(shape):
        return pl.BlockSpec(shape, lambda b, s: (0,) * len(shape),
                            pipeline_mode=pl.Buffered(1))

    in_specs = [
        pl.BlockSpec((1, tm, D_MODEL), lambda b, s: (b, s, 0)),
        pl.BlockSpec((1, 1, tm, PLE_DIM), lambda b, s: (layer, b, s, 0)),
        whole((depth, D_MODEL)),
        resident((D_MODEL, PROJ_WIDTH)),
        whole((depth, WIDTH_A)),
        whole((depth, WIDTH_A)),
        resident((HEADS_A, CHUNK, CHUNK)),
        resident((CHUNK, WIDTH_A)),
        resident((CONV_K, WIDTH_B)),
        resident((D_MODEL, D_MODEL)),
        whole((depth, D_MODEL)),
        resident((D_MODEL, D_MODEL)),
        resident((PLE_DIM, D_MODEL)),
        whole((1, D_MODEL)),
    ]
    return pl.pallas_call(
        functools.partial(_layer_kernel, tm=tm, sub=SUB_TILE, layer=layer, is_last=is_last),
        grid=(bsz, seq // tm),
        in_specs=in_specs,
        out_specs=pl.BlockSpec((1, tm, D_MODEL), lambda b, s: (b, s, 0)),
        out_shape=jax.ShapeDtypeStruct(x.shape, x.dtype),
        scratch_shapes=[
            pltpu.VMEM((V7X_SUBLANES, WIDTH_B), _F32),
        ],
        compiler_params=pltpu.CompilerParams(
            dimension_semantics=("arbitrary", "arbitrary"),
            vmem_limit_bytes=V7X_VMEM_LIMIT_BYTES),
        name=f"trunk_layer{layer}",
    )(x, p, *weights, final_g)


def kernel(x, p, norm_g, w_in, ln_v_g, ln_v_b, w_s, b_s, conv_w, w_out, ple_norm_g,
           w_ple_gate, w_ple_proj, final_g):
    depth = w_in.shape[0]
    bias = jnp.repeat(jnp.swapaxes(b_s, 1, 2), HEAD_DIM_A, axis=2)
    w_in_g, w_out_b, w_gate_b, w_proj_b = _prep_weights(w_in, w_out, w_ple_gate, w_ple_proj)
    weights = (norm_g, w_in_g, ln_v_g, ln_v_b, w_s, bias, jnp.swapaxes(conv_w, 1, 2),
               w_out_b, ple_norm_g, w_gate_b, w_proj_b)
    for layer in range(depth):
        x = _layer_call(x, p, layer, weights, final_g[None, :], is_last=layer == depth - 1)
    return x
```

```python
import functools

import jax
import jax.numpy as jnp
from jax import lax
from jax.experimental import pallas as pl
from jax.experimental.pallas import tpu as pltpu

D_MODEL = 1024
WIDTH_A = 512
WIDTH_B = 512
HEADS_A = 4
HEAD_DIM_A = WIDTH_A // HEADS_A
CHUNK = 128
CONV_K = 3
PLE_DIM = 256
EPS = 1e-6
OFF_U, OFF_V, OFF_ZA, OFF_H, OFF_GB, OFF_GC, OFF_ZB = (
    0, WIDTH_A, 2 * WIDTH_A, 3 * WIDTH_A, 3 * WIDTH_A + WIDTH_B,
    3 * WIDTH_A + 2 * WIDTH_B, 3 * WIDTH_A + 3 * WIDTH_B)
PROJ_WIDTH = 3 * WIDTH_A + 4 * WIDTH_B
GROUPED_OFF_V = 0
GROUPED_OFF_A = WIDTH_A
GROUPED_OFF_B = 3 * WIDTH_A

V7X_SUBLANES = 8
V7X_VMEM_LIMIT_BYTES = 56 * 1024 * 1024
SEQ_TILE = 1024
SUB_TILE = 256
PROJ_GROUP = 2
PREP_ROW_BLOCKS = 4

_F32 = jnp.float32
_BF16 = jnp.bfloat16


def _rmsnorm(x, g):
    return x * lax.rsqrt(jnp.mean(x * x, axis=-1, keepdims=True) + EPS) * g


def _silu(z):
    return z * jax.nn.sigmoid(z)


def _shift_rows(cur, prev_tail, k):
    rolled = pltpu.roll(cur, k, axis=0)
    head = pltpu.roll(prev_tail, k, axis=0)
    row = lax.broadcasted_iota(jnp.int32, head.shape, 0)
    first = jnp.where(row < k, head, rolled[0:V7X_SUBLANES])
    return jnp.concatenate([first, rolled[V7X_SUBLANES:]], axis=0)


def _layer_kernel(x_ref, p_ref, norm_g_ref, w_in_ref, ln_g_ref, ln_b_ref, ws_ref, bias_ref,
                  conv_w_ref, w_out_ref, ple_g_ref, w_gate_ref, w_proj_ref, final_g_ref,
                  o_ref, xc_ref, *, tm, sub, layer, is_last):
    @pl.when(pl.program_id(1) == 0)
    def _():
        xc_ref[...] = jnp.zeros((V7X_SUBLANES, WIDTH_B), _F32)

    tril = (lax.broadcasted_iota(jnp.int32, (CHUNK, CHUNK), 0)
            >= lax.broadcasted_iota(jnp.int32, (CHUNK, CHUNK), 1))
    ws = [jnp.where(tril, ws_ref[h], 0.0).astype(_BF16) for h in range(HEADS_A)]

    n_sub = tm // sub
    st = [{} for _ in range(n_sub)]

    rows_p = PROJ_GROUP * sub

    def norm_in(j):
        st[j]["x"] = x_ref[0, j * sub:(j + 1) * sub, :]
        if j % PROJ_GROUP == 0:
            x = x_ref[0, j * sub:j * sub + rows_p, :]
            st[j]["hn"] = _rmsnorm(x, norm_g_ref[layer:layer + 1, :]).astype(_BF16)

    def project(j):
        if j % PROJ_GROUP != 0:
            return
        s = st[j]

        def proj(off, width):
            return jnp.dot(s["hn"], w_in_ref[:, off:off + width],
                           preferred_element_type=_F32)

        v = proj(GROUPED_OFF_V, WIDTH_A)
        mu = jnp.mean(v, axis=-1, keepdims=True)
        vc = v - mu
        var = jnp.mean(vc * vc, axis=-1, keepdims=True)
        v_ln = vc * lax.rsqrt(var + EPS) * ln_g_ref[layer:layer + 1, :]
        v_ln = (v_ln + ln_b_ref[layer:layer + 1, :]).astype(_BF16)
        for g in range(PROJ_GROUP):
            st[j + g]["v_ln"] = v_ln[g * sub:(g + 1) * sub]
        prev_tail = xc_ref[...] if j == 0 else st[j - 1]["xc_tail"]
        tails, outs = [], []
        for c in range(WIDTH_B // CHUNK):
            lanes = slice(c * CHUNK, (c + 1) * CHUNK)
            r = proj(GROUPED_OFF_B + 4 * CHUNK * c, 4 * CHUNK)
            xc = r[:, 2 * CHUNK:3 * CHUNK] * r[:, 0:CHUNK]
            tails.append(xc[rows_p - V7X_SUBLANES:, :])
            tail_c = prev_tail[:, lanes]
            y = _shift_rows(xc, tail_c, 2) * conv_w_ref[0:1, lanes]
            y = y + _shift_rows(xc, tail_c, 1) * conv_w_ref[1:2, lanes]
            y = y + xc * conv_w_ref[2:3, lanes]
            out_c = r[:, CHUNK:2 * CHUNK] * y * _silu(r[:, 3 * CHUNK:])
            outs.append(out_c.astype(_BF16))
        st[j + PROJ_GROUP - 1]["xc_tail"] = jnp.concatenate(tails, axis=1)
        out_b = jnp.concatenate(outs, axis=1)
        for g in range(PROJ_GROUP):
            st[j + g]["out_b"] = out_b[g * sub:(g + 1) * sub]
        gates = []
        for pair in range(HEADS_A // 2):
            r = proj(GROUPED_OFF_A + 4 * HEAD_DIM_A * pair, 4 * HEAD_DIM_A)
            gates.append(r[:, 0:HEAD_DIM_A] * _silu(r[:, HEAD_DIM_A:2 * HEAD_DIM_A]))
            gates.append(r[:, 2 * HEAD_DIM_A:3 * HEAD_DIM_A] * _silu(r[:, 3 * HEAD_DIM_A:]))
        gate_a = jnp.concatenate(gates, axis=1)
        for g in range(PROJ_GROUP):
            st[j + g]["gate_a"] = gate_a[g * sub:(g + 1) * sub]

    def spatial(j):
        s = st[j]
        n_chunk = sub // CHUNK
        heads = []
        for h in range(HEADS_A):
            cols = slice(h * HEAD_DIM_A, (h + 1) * HEAD_DIM_A)
            v_h = jnp.concatenate(
                [s["v_ln"][c * CHUNK:(c + 1) * CHUNK, cols] for c in range(n_chunk)], axis=1)
            mixed = jnp.dot(ws[h], v_h, preferred_element_type=_F32)
            mixed = jnp.concatenate(
                [mixed[:, c * HEAD_DIM_A:(c + 1) * HEAD_DIM_A] for c in range(n_chunk)], axis=0)
            mixed = mixed + jnp.concatenate([bias_ref[:, cols]] * n_chunk, axis=0)
            heads.append((s["gate_a"][:, cols] * mixed).astype(_BF16))
        s["out_a"] = jnp.concatenate(heads, axis=1)

    def out_project(j):
        s = st[j]
        act = jnp.concatenate([s["out_a"], s["out_b"]], axis=1)
        s["x1"] = s["x"] + jnp.dot(act, w_out_ref[...], preferred_element_type=_F32)
        s["gn"] = _rmsnorm(s["x1"], ple_g_ref[layer:layer + 1, :]).astype(_BF16)

    def embed_project(j):
        r0 = j * sub
        st[j]["pe"] = jnp.dot(p_ref[0, 0, r0:r0 + sub, :].astype(_BF16), w_proj_ref[...],
                              preferred_element_type=_F32)

    def embed_gate(j, project_after=False):
        s, r0 = st[j], j * sub
        gate = jax.nn.sigmoid(jnp.dot(s["gn"], w_gate_ref[...], preferred_element_type=_F32))
        if project_after:
            embed_project(j)
        x2 = s["x1"] + gate * s["pe"]
        if is_last:
            x2 = _rmsnorm(x2, final_g_ref[...])
        o_ref[0, r0:r0 + sub, :] = x2

    n_lead = min(2, n_sub)
    norm_in(0)
    for j in range(n_lead):
        embed_project(j)
    project(0)
    for k in range(1, n_sub):
        norm_in(k)
        spatial(k - 1)
        project(k)
        out_project(k - 1)
        if k >= 2:
            embed_gate(k - 2)
    spatial(n_sub - 1)
    for j in range(n_lead, n_sub - 1):
        embed_project(j)
    out_project(n_sub - 1)
    if n_sub >= 2:
        embed_gate(n_sub - 2)
    embed_gate(n_sub - 1, project_after=n_sub > n_lead)

    xc_ref[...] = st[n_sub - 1]["xc_tail"]


def _grouped_sources():
    src = [OFF_V + i * CHUNK for i in range(WIDTH_A // CHUNK)]
    for h in range(HEADS_A):
        src += [OFF_U + h * HEAD_DIM_A, OFF_ZA + h * HEAD_DIM_A]
    for c in range(WIDTH_B // CHUNK):
        src += [OFF_H + c * CHUNK, OFF_GB + c * CHUNK, OFF_GC + c * CHUNK, OFF_ZB + c * CHUNK]
    return src


def _prep_kernel(w_in_ref, w_out_ref, w_gate_ref, w_proj_ref,
                 w_in_o, w_out_o, w_gate_o, w_proj_o):
    for piece, src in enumerate(_grouped_sources()):
        w_in_o[:, piece * CHUNK:(piece + 1) * CHUNK] = (
            w_in_ref[:, src:src + CHUNK].astype(_BF16))
    w_out_o[...] = w_out_ref[...].astype(_BF16)
    w_gate_o[...] = w_gate_ref[...].astype(_BF16)
    w_proj_o[...] = w_proj_ref[...].astype(_BF16)


def _prep_weights(w_in, w_out, w_gate, w_proj):
    depth = w_in.shape[0]
    n_blk = PREP_ROW_BLOCKS
    arrays = (w_in, w_out, w_gate, w_proj)

    def spec(a):
        return pl.BlockSpec((None, a.shape[1] // n_blk, a.shape[2]), lambda d, r: (d, r, 0))

    return pl.pallas_call(
        _prep_kernel,
        grid=(depth, n_blk),
        in_specs=[spec(a) for a in arrays],
        out_specs=[spec(a) for a in arrays],
        out_shape=[jax.ShapeDtypeStruct(a.shape, _BF16) for a in arrays],
        compiler_params=pltpu.CompilerParams(
            dimension_semantics=("arbitrary", "arbitrary"),
            vmem_limit_bytes=V7X_VMEM_LIMIT_BYTES),
        name="trunk_weight_prep",
    )(*arrays)


def _layer_call(x, p, layer, weights, final_g, *, is_last):
    bsz, seq, _ = x.shape
    depth = weights[0].shape[0]
    tm = SEQ_TILE
    assert seq % tm == 0 and tm % SUB_TILE == 0 and SUB_TILE % CHUNK == 0

    def resident(shape):
        return pl.BlockSpec((None,) + shape, lambda b, s: (layer,) + (0,) * len(shape),
                            pipeline_mode=pl.Buffered(1))

    def whole(shape):
        return pl.BlockSpec(shape, lambda b, s: (0,) * len(shape),
                            pipeline_mode=pl.Buffered(1))

    in_specs = [
        pl.BlockSpec((1, tm, D_MODEL), lambda b, s: (b, s, 0)),
        pl.BlockSpec((1, 1, tm, PLE_DIM), lambda b, s: (layer, b, s, 0)),
        whole((depth, D_MODEL)),
        resident((D_MODEL, PROJ_WIDTH)),
        whole((depth, WIDTH_A)),
        whole((depth, WIDTH_A)),
        resident((HEADS_A, CHUNK, CHUNK)),
        resident((CHUNK, WIDTH_A)),
        resident((CONV_K, WIDTH_B)),
        resident((D_MODEL, D_MODEL)),
        whole((depth, D_MODEL)),
        resident((D_MODEL, D_MODEL)),
        resident((PLE_DIM, D_MODEL)),
        whole((1, D_MODEL)),
    ]
    return pl.pallas_call(
        functools.partial(_layer_kernel, tm=tm, sub=SUB_TILE, layer=layer, is_last=is_last),
        grid=(bsz, seq // tm),
        in_specs=in_specs,
        out_specs=pl.BlockSpec((1, tm, D_MODEL), lambda b, s: (b, s, 0)),
        out_shape=jax.ShapeDtypeStruct(x.shape, x.dtype),
        scratch_shapes=[
            pltpu.VMEM((V7X_SUBLANES, WIDTH_B), _F32),
        ],
        compiler_params=pltpu.CompilerParams(
            dimension_semantics=("arbitrary", "arbitrary"),
            vmem_limit_bytes=V7X_VMEM_LIMIT_BYTES),
        name=f"trunk_layer{layer}",
    )(x, p, *weights, final_g)


def kernel(x, p, norm_g, w_in, ln_v_g, ln_v_b, w_s, b_s, conv_w, w_out, ple_norm_g,
           w_ple_gate, w_ple_proj, final_g):
    depth = w_in.shape[0]
    bias = jnp.repeat(jnp.swapaxes(b_s, 1, 2), HEAD_DIM_A, axis=2)
    w_in_g, w_out_b, w_gate_b, w_proj_b = _prep_weights(w_in, w_out, w_ple_gate, w_ple_proj)
    weights = (norm_g, w_in_g, ln_v_g, ln_v_b, w_s, bias, jnp.swapaxes(conv_w, 1, 2),
               w_out_b, ple_norm_g, w_gate_b, w_proj_b)
    for layer in range(depth):
        x = _layer_call(x, p, layer, weights, final_g[None, :], is_last=layer == depth - 1)
    return x
```

```python
import functools

import jax
import jax.numpy as jnp
from jax import lax
from jax.experimental import pallas as pl
from jax.experimental.pallas import tpu as pltpu

D_MODEL = 1024
WIDTH_A = 512
WIDTH_B = 512
HEADS_A = 4
HEAD_DIM_A = WIDTH_A // HEADS_A
CHUNK = 128
CONV_K = 3
PLE_DIM = 256
EPS = 1e-6
OFF_U, OFF_V, OFF_ZA, OFF_H, OFF_GB, OFF_GC, OFF_ZB = (
    0, WIDTH_A, 2 * WIDTH_A, 3 * WIDTH_A, 3 * WIDTH_A + WIDTH_B,
    3 * WIDTH_A + 2 * WIDTH_B, 3 * WIDTH_A + 3 * WIDTH_B)
PROJ_WIDTH = 3 * WIDTH_A + 4 * WIDTH_B
GROUPED_OFF_V = 0
GROUPED_OFF_A = WIDTH_A
GROUPED_OFF_B = 3 * WIDTH_A

V7X_SUBLANES = 8
V7X_VMEM_LIMIT_BYTES = 56 * 1024 * 1024
SEQ_TILE = 1024
SUB_TILE = 256
PREP_ROW_BLOCKS = 4

_F32 = jnp.float32
_BF16 = jnp.bfloat16


def _rmsnorm(x, g):
    return x * lax.rsqrt(jnp.mean(x * x, axis=-1, keepdims=True) + EPS) * g


def _silu(z):
    return z * jax.nn.sigmoid(z)


def _shift_rows(cur, prev_tail, k):
    rolled = pltpu.roll(cur, k, axis=0)
    head = pltpu.roll(prev_tail, k, axis=0)
    row = lax.broadcasted_iota(jnp.int32, head.shape, 0)
    first = jnp.where(row < k, head, rolled[0:V7X_SUBLANES])
    return jnp.concatenate([first, rolled[V7X_SUBLANES:]], axis=0)


def _layer_kernel(x_ref, p_ref, norm_g_ref, w_in_ref, ln_g_ref, ln_b_ref, ws_ref, bias_ref,
                  conv_w_ref, w_out_ref, ple_g_ref, w_gate_ref, w_proj_ref, final_g_ref,
                  o_ref, xc_ref, *, tm, sub, layer, is_last):
    @pl.when(pl.program_id(1) == 0)
    def _():
        xc_ref[...] = jnp.zeros((V7X_SUBLANES, WIDTH_B), _F32)

    tril = (lax.broadcasted_iota(jnp.int32, (CHUNK, CHUNK), 0)
            >= lax.broadcasted_iota(jnp.int32, (CHUNK, CHUNK), 1))
    ws = [jnp.where(tril, ws_ref[h], 0.0).astype(_BF16) for h in range(HEADS_A)]

    n_sub = tm // sub
    st = [{} for _ in range(n_sub)]

    def norm_in(j):
        s = st[j]
        s["x"] = x_ref[0, j * sub:(j + 1) * sub, :]
        s["hn"] = _rmsnorm(s["x"], norm_g_ref[layer:layer + 1, :]).astype(_BF16)

    def project(j):
        s = st[j]

        def proj(off, width):
            return jnp.dot(s["hn"], w_in_ref[:, off:off + width],
                           preferred_element_type=_F32)

        v = proj(GROUPED_OFF_V, WIDTH_A)
        mu = jnp.mean(v, axis=-1, keepdims=True)
        vc = v - mu
        var = jnp.mean(vc * vc, axis=-1, keepdims=True)
        v_ln = vc * lax.rsqrt(var + EPS) * ln_g_ref[layer:layer + 1, :]
        s["v_ln"] = (v_ln + ln_b_ref[layer:layer + 1, :]).astype(_BF16)
        prev_tail = xc_ref[...] if j == 0 else st[j - 1]["xc_tail"]
        tails, outs = [], []
        for c in range(WIDTH_B // CHUNK):
            lanes = slice(c * CHUNK, (c + 1) * CHUNK)
            r = proj(GROUPED_OFF_B + 4 * CHUNK * c, 4 * CHUNK)
            xc = r[:, 2 * CHUNK:3 * CHUNK] * r[:, 0:CHUNK]
            tails.append(xc[sub - V7X_SUBLANES:, :])
            tail_c = prev_tail[:, lanes]
            y = _shift_rows(xc, tail_c, 2) * conv_w_ref[0:1, lanes]
            y = y + _shift_rows(xc, tail_c, 1) * conv_w_ref[1:2, lanes]
            y = y + xc * conv_w_ref[2:3, lanes]
            out_c = r[:, CHUNK:2 * CHUNK] * y * _silu(r[:, 3 * CHUNK:])
            outs.append(out_c.astype(_BF16))
        s["xc_tail"] = jnp.concatenate(tails, axis=1)
        s["out_b"] = jnp.concatenate(outs, axis=1)
        gates = []
        for pair in range(HEADS_A // 2):
            r = proj(GROUPED_OFF_A + 4 * HEAD_DIM_A * pair, 4 * HEAD_DIM_A)
            gates.append(r[:, 0:HEAD_DIM_A] * _silu(r[:, HEAD_DIM_A:2 * HEAD_DIM_A]))
            gates.append(r[:, 2 * HEAD_DIM_A:3 * HEAD_DIM_A] * _silu(r[:, 3 * HEAD_DIM_A:]))
        s["gate_a"] = jnp.concatenate(gates, axis=1)

    def spatial(j):
        s = st[j]
        n_chunk = sub // CHUNK
        heads = []
        for h in range(HEADS_A):
            cols = slice(h * HEAD_DIM_A, (h + 1) * HEAD_DIM_A)
            v_h = jnp.concatenate(
                [s["v_ln"][c * CHUNK:(c + 1) * CHUNK, cols] for c in range(n_chunk)], axis=1)
            mixed = jnp.dot(ws[h], v_h, preferred_element_type=_F32)
            mixed = jnp.concatenate(
                [mixed[:, c * HEAD_DIM_A:(c + 1) * HEAD_DIM_A] for c in range(n_chunk)], axis=0)
            mixed = mixed + jnp.concatenate([bias_ref[:, cols]] * n_chunk, axis=0)
            heads.append((s["gate_a"][:, cols] * mixed).astype(_BF16))
        s["out_a"] = jnp.concatenate(heads, axis=1)

    def out_project(j):
        s = st[j]
        act = jnp.concatenate([s["out_a"], s["out_b"]], axis=1)
        s["x1"] = s["x"] + jnp.dot(act, w_out_ref[...], preferred_element_type=_F32)
        s["gn"] = _rmsnorm(s["x1"], ple_g_ref[layer:layer + 1, :]).astype(_BF16)

    def embed_project(j):
        r0 = j * sub
        st[j]["pe"] = jnp.dot(p_ref[0, 0, r0:r0 + sub, :].astype(_BF16), w_proj_ref[...],
                              preferred_element_type=_F32)

    def embed_gate(j, project_after=False):
        s, r0 = st[j], j * sub
        gate = jax.nn.sigmoid(jnp.dot(s["gn"], w_gate_ref[...], preferred_element_type=_F32))
        if project_after:
            embed_project(j)
        x2 = s["x1"] + gate * s["pe"]
        if is_last:
            x2 = _rmsnorm(x2, final_g_ref[...])
        o_ref[0, r0:r0 + sub, :] = x2

    n_lead = min(2, n_sub)
    norm_in(0)
    for j in range(n_lead):
        embed_project(j)
    project(0)
    for k in range(1, n_sub):
        norm_in(k)
        spatial(k - 1)
        project(k)
        if k >= 2:
            embed_gate(k - 2)
        out_project(k - 1)
    spatial(n_sub - 1)
    for j in range(n_lead, n_sub - 1):
        embed_project(j)
    out_project(n_sub - 1)
    if n_sub >= 2:
        embed_gate(n_sub - 2)
    embed_gate(n_sub - 1, project_after=n_sub > n_lead)

    xc_ref[...] = st[n_sub - 1]["xc_tail"]


def _grouped_sources():
    src = [OFF_V + i * CHUNK for i in range(WIDTH_A // CHUNK)]
    for h in range(HEADS_A):
        src += [OFF_U + h * HEAD_DIM_A, OFF_ZA + h * HEAD_DIM_A]
    for c in range(WIDTH_B // CHUNK):
        src += [OFF_H + c * CHUNK, OFF_GB + c * CHUNK, OFF_GC + c * CHUNK, OFF_ZB + c * CHUNK]
    return src


def _prep_kernel(w_in_ref, w_out_ref, w_gate_ref, w_proj_ref,
                 w_in_o, w_out_o, w_gate_o, w_proj_o):
    for piece, src in enumerate(_grouped_sources()):
        w_in_o[:, piece * CHUNK:(piece + 1) * CHUNK] = (
            w_in_ref[:, src:src + CHUNK].astype(_BF16))
    w_out_o[...] = w_out_ref[...].astype(_BF16)
    w_gate_o[...] = w_gate_ref[...].astype(_BF16)
    w_proj_o[...] = w_proj_ref[...].astype(_BF16)


def _prep_weights(w_in, w_out, w_gate, w_proj):
    depth = w_in.shape[0]
    n_blk = PREP_ROW_BLOCKS
    arrays = (w_in, w_out, w_gate, w_proj)

    def spec(a):
        return pl.BlockSpec((None, a.shape[1] // n_blk, a.shape[2]), lambda d, r: (d, r, 0))

    return pl.pallas_call(
        _prep_kernel,
        grid=(depth, n_blk),
        in_specs=[spec(a) for a in arrays],
        out_specs=[spec(a) for a in arrays],
        out_shape=[jax.ShapeDtypeStruct(a.shape, _BF16) for a in arrays],
        compiler_params=pltpu.CompilerParams(
            dimension_semantics=("arbitrary", "arbitrary"),
            vmem_limit_bytes=V7X_VMEM_LIMIT_BYTES),
        name="trunk_weight_prep",
    )(*arrays)


def _layer_call(x, p, layer, weights, final_g, *, is_last):
    bsz, seq, _ = x.shape
    depth = weights[0].shape[0]
    tm = SEQ_TILE
    assert seq % tm == 0 and tm % SUB_TILE == 0 and SUB_TILE % CHUNK == 0

    def resident(shape):
        return pl.BlockSpec((None,) + shape, lambda b, s: (layer,) + (0,) * len(shape),
                            pipeline_mode=pl.Buffered(1))

    def whole(shape):
        return pl.BlockSpec(shape, lambda b, s: (0,) * len(shape),
                            pipeline_mode=pl.Buffered(1))

    in_specs = [
        pl.BlockSpec((1, tm, D_MODEL), lambda b, s: (b, s, 0)),
        pl.BlockSpec((1, 1, tm, PLE_DIM), lambda b, s: (layer, b, s, 0)),
        whole((depth, D_MODEL)),
        resident((D_MODEL, PROJ_WIDTH)),
        whole((depth, WIDTH_A)),
        whole((depth, WIDTH_A)),
        resident((HEADS_A, CHUNK, CHUNK)),
        resident((CHUNK, WIDTH_A)),
        resident((CONV_K, WIDTH_B)),
        resident((D_MODEL, D_MODEL)),
        whole((depth, D_MODEL)),
        resident((D_MODEL, D_MODEL)),
        resident((PLE_DIM, D_MODEL)),
        whole((1, D_MODEL)),
    ]
    return pl.pallas_call(
        functools.partial(_layer_kernel, tm=tm, sub=SUB_TILE, layer=layer, is_last=is_last),
        grid=(bsz, seq // tm),
        in_specs=in_specs,
        out_specs=pl.BlockSpec((1, tm, D_MODEL), lambda b, s: (b, s, 0)),
        out_shape=jax.ShapeDtypeStruct(x.shape, x.dtype),
        scratch_shapes=[
            pltpu.VMEM((V7X_SUBLANES, WIDTH_B), _F32),
        ],
        compiler_params=pltpu.CompilerParams(
            dimension_semantics=("arbitrary", "arbitrary"),
            vmem_limit_bytes=V7X_VMEM_LIMIT_BYTES),
        name=f"trunk_layer{layer}",
    )(x, p, *weights, final_g)


def kernel(x, p, norm_g, w_in, ln_v_g, ln_v_b, w_s, b_s, conv_w, w_out, ple_norm_g,
           w_ple_gate, w_ple_proj, final_g):
    depth = w_in.shape[0]
    bias = jnp.repeat(jnp.swapaxes(b_s, 1, 2), HEAD_DIM_A, axis=2)
    w_in_g, w_out_b, w_gate_b, w_proj_b = _prep_weights(w_in, w_out, w_ple_gate, w_ple_proj)
    weights = (norm_g, w_in_g, ln_v_g, ln_v_b, w_s, bias, jnp.swapaxes(conv_w, 1, 2),
               w_out_b, ple_norm_g, w_gate_b, w_proj_b)
    for layer in range(depth):
        x = _layer_call(x, p, layer, weights, final_g[None, :], is_last=layer == depth - 1)
    return x
```

```python
import functools

import jax
import jax.numpy as jnp
from jax import lax
from jax.experimental import pallas as pl
from jax.experimental.pallas import tpu as pltpu

D_MODEL = 1024
WIDTH_A = 512
WIDTH_B = 512
HEADS_A = 4
HEAD_DIM_A = WIDTH_A // HEADS_A
CHUNK = 128
CONV_K = 3
PLE_DIM = 256
EPS = 1e-6
OFF_U, OFF_V, OFF_ZA, OFF_H, OFF_GB, OFF_GC, OFF_ZB = (
    0, WIDTH_A, 2 * WIDTH_A, 3 * WIDTH_A, 3 * WIDTH_A + WIDTH_B,
    3 * WIDTH_A + 2 * WIDTH_B, 3 * WIDTH_A + 3 * WIDTH_B)
PROJ_WIDTH = 3 * WIDTH_A + 4 * WIDTH_B
GROUPED_OFF_V = 0
GROUPED_OFF_A = WIDTH_A
GROUPED_OFF_B = 3 * WIDTH_A

V7X_SUBLANES = 8
V7X_VMEM_LIMIT_BYTES = 56 * 1024 * 1024
SEQ_TILE = 1024
SUB_TILE = 256
PREP_ROW_BLOCKS = 4

_F32 = jnp.float32
_BF16 = jnp.bfloat16


def _rmsnorm(x, g):
    return x * lax.rsqrt(jnp.mean(x * x, axis=-1, keepdims=True) + EPS) * g


def _silu(z):
    return z * jax.nn.sigmoid(z)


def _shift_rows(cur, prev_tail, k):
    rolled = pltpu.roll(cur, k, axis=0)
    head = pltpu.roll(prev_tail, k, axis=0)
    row = lax.broadcasted_iota(jnp.int32, head.shape, 0)
    first = jnp.where(row < k, head, rolled[0:V7X_SUBLANES])
    return jnp.concatenate([first, rolled[V7X_SUBLANES:]], axis=0)


def _layer_kernel(x_ref, p_ref, norm_g_ref, w_in_ref, ln_g_ref, ln_b_ref, ws_ref, bias_ref,
                  conv_w_ref, w_out_ref, ple_g_ref, w_gate_ref, w_proj_ref, final_g_ref,
                  o_ref, xc_ref, *, tm, sub, layer, is_last):
    @pl.when(pl.program_id(1) == 0)
    def _():
        xc_ref[...] = jnp.zeros((V7X_SUBLANES, WIDTH_B), _F32)

    tril = (lax.broadcasted_iota(jnp.int32, (CHUNK, CHUNK), 0)
            >= lax.broadcasted_iota(jnp.int32, (CHUNK, CHUNK), 1))

    n_sub = tm // sub
    st = [{} for _ in range(n_sub)]

    def norm_in(j):
        s = st[j]
        s["x"] = x_ref[0, j * sub:(j + 1) * sub, :]
        s["hn"] = _rmsnorm(s["x"], norm_g_ref[layer:layer + 1, :]).astype(_BF16)

    def project(j):
        s = st[j]

        def proj(off, width):
            return jnp.dot(s["hn"], w_in_ref[:, off:off + width],
                           preferred_element_type=_F32)

        v = proj(GROUPED_OFF_V, WIDTH_A)
        mu = jnp.mean(v, axis=-1, keepdims=True)
        vc = v - mu
        var = jnp.mean(vc * vc, axis=-1, keepdims=True)
        v_ln = vc * lax.rsqrt(var + EPS) * ln_g_ref[layer:layer + 1, :]
        s["v_ln"] = (v_ln + ln_b_ref[layer:layer + 1, :]).astype(_BF16)
        prev_tail = xc_ref[...] if j == 0 else st[j - 1]["xc_tail"]
        tails, outs = [], []
        for c in range(WIDTH_B // CHUNK):
            lanes = slice(c * CHUNK, (c + 1) * CHUNK)
            r = proj(GROUPED_OFF_B + 4 * CHUNK * c, 4 * CHUNK)
            xc = r[:, 2 * CHUNK:3 * CHUNK] * r[:, 0:CHUNK]
            tails.append(xc[sub - V7X_SUBLANES:, :])
            tail_c = prev_tail[:, lanes]
            y = _shift_rows(xc, tail_c, 2) * conv_w_ref[0:1, lanes]
            y = y + _shift_rows(xc, tail_c, 1) * conv_w_ref[1:2, lanes]
            y = y + xc * conv_w_ref[2:3, lanes]
            out_c = r[:, CHUNK:2 * CHUNK] * y * _silu(r[:, 3 * CHUNK:])
            outs.append(out_c.astype(_BF16))
        s["xc_tail"] = jnp.concatenate(tails, axis=1)
        s["out_b"] = jnp.concatenate(outs, axis=1)
        gates = []
        for pair in range(HEADS_A // 2):
            r = proj(GROUPED_OFF_A + 4 * HEAD_DIM_A * pair, 4 * HEAD_DIM_A)
            gates.append(r[:, 0:HEAD_DIM_A] * _silu(r[:, HEAD_DIM_A:2 * HEAD_DIM_A]))
            gates.append(r[:, 2 * HEAD_DIM_A:3 * HEAD_DIM_A] * _silu(r[:, 3 * HEAD_DIM_A:]))
        s["gate_a"] = jnp.concatenate(gates, axis=1)

    def spatial(j):
        s = st[j]
        n_chunk = sub // CHUNK
        heads = []
        for h in range(HEADS_A):
            cols = slice(h * HEAD_DIM_A, (h + 1) * HEAD_DIM_A)
            v_h = jnp.concatenate(
                [s["v_ln"][c * CHUNK:(c + 1) * CHUNK, cols] for c in range(n_chunk)], axis=1)
            w_h = jnp.where(tril, ws_ref[h], 0.0).astype(_BF16)
            mixed = jnp.dot(w_h, v_h, preferred_element_type=_F32)
            mixed = jnp.concatenate(
                [mixed[:, c * HEAD_DIM_A:(c + 1) * HEAD_DIM_A] for c in range(n_chunk)], axis=0)
            mixed = mixed + jnp.concatenate([bias_ref[:, cols]] * n_chunk, axis=0)
            heads.append((s["gate_a"][:, cols] * mixed).astype(_BF16))
        s["out_a"] = jnp.concatenate(heads, axis=1)

    def out_project(j):
        s = st[j]
        act = jnp.concatenate([s["out_a"], s["out_b"]], axis=1)
        s["x1"] = s["x"] + jnp.dot(act, w_out_ref[...], preferred_element_type=_F32)
        s["gn"] = _rmsnorm(s["x1"], ple_g_ref[layer:layer + 1, :]).astype(_BF16)

    def embed_project(j):
        r0 = j * sub
        st[j]["pe"] = jnp.dot(p_ref[0, 0, r0:r0 + sub, :].astype(_BF16), w_proj_ref[...],
                              preferred_element_type=_F32)

    def embed_gate(j, project_after=False):
        s, r0 = st[j], j * sub
        gate = jax.nn.sigmoid(jnp.dot(s["gn"], w_gate_ref[...], preferred_element_type=_F32))
        if project_after:
            embed_project(j)
        x2 = s["x1"] + gate * s["pe"]
        if is_last:
            x2 = _rmsnorm(x2, final_g_ref[...])
        o_ref[0, r0:r0 + sub, :] = x2

    n_lead = min(2, n_sub)
    norm_in(0)
    for j in range(n_lead):
        embed_project(j)
    project(0)
    for k in range(1, n_sub):
        norm_in(k)
        spatial(k - 1)
        project(k)
        if k >= 2:
            embed_gate(k - 2)
        out_project(k - 1)
    spatial(n_sub - 1)
    for j in range(n_lead, n_sub - 1):
        embed_project(j)
    out_project(n_sub - 1)
    if n_sub >= 2:
        embed_gate(n_sub - 2)
    embed_gate(n_sub - 1, project_after=n_sub > n_lead)

    xc_ref[...] = st[n_sub - 1]["xc_tail"]


def _grouped_sources():
    src = [OFF_V + i * CHUNK for i in range(WIDTH_A // CHUNK)]
    for h in range(HEADS_A):
        src += [OFF_U + h * HEAD_DIM_A, OFF_ZA + h * HEAD_DIM_A]
    for c in range(WIDTH_B // CHUNK):
        src += [OFF_H + c * CHUNK, OFF_GB + c * CHUNK, OFF_GC + c * CHUNK, OFF_ZB + c * CHUNK]
    return src


def _prep_kernel(w_in_ref, w_out_ref, w_gate_ref, w_proj_ref,
                 w_in_o, w_out_o, w_gate_o, w_proj_o):
    for piece, src in enumerate(_grouped_sources()):
        w_in_o[:, piece * CHUNK:(piece + 1) * CHUNK] = (
            w_in_ref[:, src:src + CHUNK].astype(_BF16))
    w_out_o[...] = w_out_ref[...].astype(_BF16)
    w_gate_o[...] = w_gate_ref[...].astype(_BF16)
    w_proj_o[...] = w_proj_ref[...].astype(_BF16)


def _prep_weights(w_in, w_out, w_gate, w_proj):
    depth = w_in.shape[0]
    n_blk = PREP_ROW_BLOCKS
    arrays = (w_in, w_out, w_gate, w_proj)

    def spec(a):
        return pl.BlockSpec((None, a.shape[1] // n_blk, a.shape[2]), lambda d, r: (d, r, 0))

    return pl.pallas_call(
        _prep_kernel,
        grid=(depth, n_blk),
        in_specs=[spec(a) for a in arrays],
        out_specs=[spec(a) for a in arrays],
        out_shape=[jax.ShapeDtypeStruct(a.shape, _BF16) for a in arrays],
        compiler_params=pltpu.CompilerParams(
            dimension_semantics=("arbitrary", "arbitrary"),
            vmem_limit_bytes=V7X_VMEM_LIMIT_BYTES),
        name="trunk_weight_prep",
    )(*arrays)


def _layer_call(x, p, layer, weights, final_g, *, is_last):
    bsz, seq, _ = x.shape
    depth = weights[0].shape[0]
    tm = SEQ_TILE
    assert seq % tm == 0 and tm % SUB_TILE == 0 and SUB_TILE % CHUNK == 0

    def resident(shape):
        return pl.BlockSpec((None,) + shape, lambda b, s: (layer,) + (0,) * len(shape),
                            pipeline_mode=pl.Buffered(1))

    def whole(shape):
        return pl.BlockSpec(shape, lambda b, s: (0,) * len(shape),
                            pipeline_mode=pl.Buffered(1))

    in_specs = [
        pl.BlockSpec((1, tm, D_MODEL), lambda b, s: (b, s, 0)),
        pl.BlockSpec((1, 1, tm, PLE_DIM), lambda b, s: (layer, b, s, 0)),
        whole((depth, D_MODEL)),
        resident((D_MODEL, PROJ_WIDTH)),
        whole((depth, WIDTH_A)),
        whole((depth, WIDTH_A)),
        resident((HEADS_A, CHUNK, CHUNK)),
        resident((CHUNK, WIDTH_A)),
        resident((CONV_K, WIDTH_B)),
        resident((D_MODEL, D_MODEL)),
        whole((depth, D_MODEL)),
        resident((D_MODEL, D_MODEL)),
        resident((PLE_DIM, D_MODEL)),
        whole((1, D_MODEL)),
    ]
    return pl.pallas_call(
        functools.partial(_layer_kernel, tm=tm, sub=SUB_TILE, layer=layer, is_last=is_last),
        grid=(bsz, seq // tm),
        in_specs=in_specs,
        out_specs=pl.BlockSpec((1, tm, D_MODEL), lambda b, s: (b, s, 0)),
        out_shape=jax.ShapeDtypeStruct(x.shape, x.dtype),
        scratch_shapes=[
            pltpu.VMEM((V7X_SUBLANES, WIDTH_B), _F32),
        ],
        compiler_params=pltpu.CompilerParams(
            dimension_semantics=("arbitrary", "arbitrary"),
            vmem_limit_bytes=V7X_VMEM_LIMIT_BYTES),
        name=f"trunk_layer{layer}",
    )(x, p, *weights, final_g)


def kernel(x, p, norm_g, w_in, ln_v_g, ln_v_b, w_s, b_s, conv_w, w_out, ple_norm_g,
           w_ple_gate, w_ple_proj, final_g):
    depth = w_in.shape[0]
    bias = jnp.repeat(jnp.swapaxes(b_s, 1, 2), HEAD_DIM_A, axis=2)
    w_in_g, w_out_b, w_gate_b, w_proj_b = _prep_weights(w_in, w_out, w_ple_gate, w_ple_proj)
    weights = (norm_g, w_in_g, ln_v_g, ln_v_b, w_s, bias, jnp.swapaxes(conv_w, 1, 2),
               w_out_b, ple_norm_g, w_gate_b, w_proj_b)
    for layer in range(depth):
        x = _layer_call(x, p, layer, weights, final_g[None, :], is_last=layer == depth - 1)
    return x
```
